```python
import math
import jax, jax.numpy as jnp
from jax import lax
import numpy as np

D_MODEL = 2048
BATCH = 4
SEQ = 2048
DEPTH = 1
DEC_BATCH = 2
DEC_SEQ = 4096
PAST_LEN = 128

EPS = 1e-6
ROPE_THETA = 10000.0
MLA_HEADS = 8
Q_LORA = 512
KV_LORA = 512
NOPE_DIM = 128
ROPE_DIM = 64
V_DIM = 128
MLA_WIDTH = MLA_HEADS * V_DIM
DIL_HEADS = 8
DIL_HD = 128
DIL_WIDTH = DIL_HEADS * DIL_HD
DIL_PATTERNS = ((128, 1), (512, 4), (2048, 16))
MIX_WIDTH = MLA_WIDTH + DIL_WIDTH
IN_SPLITS = (Q_LORA, KV_LORA, ROPE_DIM, DIL_WIDTH, DIL_WIDTH, DIL_WIDTH)
IN_COLS = sum(IN_SPLITS)
D_FF = 4 * D_MODEL
Q_BLOCK = 128

kernel_name = "hybrid_mla_dilated_encoder"


def rmsnorm(x, g):
    xf = x.astype(jnp.float32)
    y = xf * lax.rsqrt(jnp.mean(xf * xf, axis=-1, keepdims=True) + EPS)
    return (y * g.astype(jnp.float32)).astype(x.dtype)


def rope(x):
    s, dim = x.shape[1], x.shape[-1]
    pos = jnp.arange(s, dtype=jnp.float32)
    inv = 1.0 / (ROPE_THETA ** (jnp.arange(0, dim, 2, dtype=jnp.float32) / dim))
    ang = pos[:, None] * inv[None, :]
    shape = (1, s) + (1,) * (x.ndim - 3) + (dim // 2,)
    c, sn = jnp.cos(ang).reshape(shape), jnp.sin(ang).reshape(shape)
    xf = x.astype(jnp.float32)
    x1, x2 = xf[..., : dim // 2], xf[..., dim // 2:]
    return jnp.concatenate([x1 * c - x2 * sn, x1 * sn + x2 * c], axis=-1).astype(x.dtype)


def mla(q_a, kv_a, k_rope_in, q_a_g, w_q_b, kv_a_g, w_kv_b):
    b, s, _ = q_a.shape
    q = (rmsnorm(q_a, q_a_g) @ w_q_b).reshape(b, s, MLA_HEADS, NOPE_DIM + ROPE_DIM)
    q_nope, q_rope = q[..., :NOPE_DIM], rope(q[..., NOPE_DIM:])
    kv = (rmsnorm(kv_a, kv_a_g) @ w_kv_b).reshape(b, s, MLA_HEADS, NOPE_DIM + V_DIM)
    k_nope, v = kv[..., :NOPE_DIM], kv[..., NOPE_DIM:]
    k_rope = rope(k_rope_in)
    scale = 1.0 / math.sqrt(NOPE_DIM + ROPE_DIM)
    nq = s // Q_BLOCK

    def to_blocks(t):
        return jnp.moveaxis(t.reshape(b, nq, Q_BLOCK, MLA_HEADS, t.shape[-1]), 1, 0)

    def block(args):
        qn, qr = args
        sc = (jnp.einsum('bqhd,bkhd->bhqk', qn, k_nope).astype(jnp.float32)
              + jnp.einsum('bqhr,bkr->bhqk', qr, k_rope).astype(jnp.float32)) * scale
        p = jax.nn.softmax(sc, axis=-1).astype(v.dtype)
        return jnp.einsum('bhqk,bkhd->bqhd', p, v)

    o = lax.map(block, (to_blocks(q_nope), to_blocks(q_rope)))
    return jnp.moveaxis(o, 0, 1).reshape(b, s, MLA_WIDTH)


def dilated_pattern(q, k, v, window, dilation):
    b, s, h, hd = q.shape
    half = window // (2 * dilation)
    L = s // dilation
    nb = -(-L // half)
    pad = nb * half - L

    def to_sub(t):
        return t.reshape(b, L, dilation, h, hd).transpose(0, 2, 1, 3, 4)

    qb = jnp.pad(to_sub(q), ((0, 0), (0, 0), (0, pad), (0, 0), (0, 0))).reshape(b, dilation, nb, half, h, hd)

    def key_windows(t):
        tp = jnp.pad(to_sub(t), ((0, 0), (0, 0), (half, pad + half), (0, 0), (0, 0)))
        tp = tp.reshape(b, dilation, nb + 2, half, h, hd)
        return jnp.concatenate([tp[:, :, :-2], tp[:, :, 1:-1], tp[:, :, 2:]], axis=3)

    kw, vw = key_windows(k), key_windows(v)
    sc = jnp.einsum('brnqhd,brnkhd->brnhqk', qb, kw).astype(jnp.float32) / math.sqrt(hd)
    nidx = jnp.arange(nb)
    q_pos = nidx[:, None] * half + jnp.arange(half)[None, :]
    k_pos = (nidx[:, None] - 1) * half + jnp.arange(3 * half)[None, :]
    rel = k_pos[:, None, :] - q_pos[:, :, None]
    valid = (jnp.abs(rel) <= half) & (k_pos[:, None, :] >= 0) & (k_pos[:, None, :] < L)
    sc = jnp.where(valid[None, None, :, None], sc, jnp.float32(-jnp.inf))
    m = jnp.max(sc, axis=-1, keepdims=True)
    e = jnp.exp(sc - m)
    den = jnp.sum(e, axis=-1, keepdims=True)
    lse = m + jnp.log(den)
    o = jnp.einsum('brnhqk,brnkhd->brnqhd', (e / den).astype(v.dtype), vw)
    o = o.reshape(b, dilation, nb * half, h, hd)[:, :, :L].transpose(0, 2, 1, 3, 4).reshape(b, s, h, hd)
    lse = lse[..., 0].transpose(0, 1, 2, 4, 3).reshape(b, dilation, nb * half, h)[:, :, :L]
    lse = lse.transpose(0, 2, 1, 3).reshape(b, s, h)
    return o, lse


def dilated_attention(qd, kd, vd):
    b, s, _ = qd.shape
    q = rope(qd.reshape(b, s, DIL_HEADS, DIL_HD))
    k = rope(kd.reshape(b, s, DIL_HEADS, DIL_HD))
    v = vd.reshape(b, s, DIL_HEADS, DIL_HD)
    outs, lses = [], []
    for window, dilation in DIL_PATTERNS:
        o, l = dilated_pattern(q, k, v, window, dilation)
        outs.append(o.astype(jnp.float32))
        lses.append(l)
    w = jax.nn.softmax(jnp.stack(lses, axis=0), axis=0)
    o = jnp.sum(w[..., None] * jnp.stack(outs, axis=0), axis=0)
    return o.astype(qd.dtype).reshape(b, s, DIL_WIDTH)


def trunk(x, attn_norm_g, w_in, q_a_norm_g, w_q_b, kv_a_norm_g, w_kv_b,
          mla_out_norm_g, dil_out_norm_g, w_o, mlp_norm_g, w_up, w_down, final_norm_g):
    offs = np.cumsum(IN_SPLITS)[:-1].tolist()
    for l in range(DEPTH):
        h = rmsnorm(x, attn_norm_g[l])
        q_a, kv_a, k_r, qd, kd, vd = jnp.split(h @ w_in[l], offs, axis=-1)
        o_a = mla(q_a, kv_a, k_r, q_a_norm_g[l], w_q_b[l], kv_a_norm_g[l], w_kv_b[l])
        o_b = dilated_attention(qd, kd, vd)
        mix = jnp.concatenate([rmsnorm(o_a, mla_out_norm_g[l]), rmsnorm(o_b, dil_out_norm_g[l])], axis=-1)
        x = x + mix @ w_o[l]
        h = rmsnorm(x, mlp_norm_g[l])
        x = x + jnp.square(jax.nn.relu(h @ w_up[l])) @ w_down[l]
    return rmsnorm(x, final_norm_g)


def setup_inputs(seed: int = 0) -> dict:
    key = jax.random.key(seed)
    ks = jax.random.split(key, 16)

    def w(k, shape):
        return jax.random.normal(k, shape, jnp.float32) * shape[-2] ** -0.5

    def g(k, shape):
        return 1.0 + 0.02 * jax.random.normal(k, shape, jnp.float32)

    return {
        "x_prompt": jax.random.normal(ks[0], (BATCH, SEQ, D_MODEL), jnp.float32),
        "x_sample": jax.random.normal(ks[1], (DEC_BATCH, DEC_SEQ, D_MODEL), jnp.float32),
        "attn_norm_g": g(ks[2], (DEPTH, D_MODEL)),
        "w_in": w(ks[3], (DEPTH, D_MODEL, IN_COLS)),
        "q_a_norm_g": g(ks[4], (DEPTH, Q_LORA)),
        "w_q_b": w(ks[5], (DEPTH, Q_LORA, MLA_HEADS * (NOPE_DIM + ROPE_DIM))),
        "kv_a_norm_g": g(ks[6], (DEPTH, KV_LORA)),
        "w_kv_b": w(ks[7], (DEPTH, KV_LORA, MLA_HEADS * (NOPE_DIM + V_DIM))),
        "mla_out_norm_g": g(ks[8], (DEPTH, MLA_WIDTH)),
        "dil_out_norm_g": g(ks[9], (DEPTH, DIL_WIDTH)),
        "w_o": w(ks[10], (DEPTH, MIX_WIDTH, D_MODEL)),
        "mlp_norm_g": g(ks[11], (DEPTH, D_MODEL)),
        "w_up": w(ks[12], (DEPTH, D_MODEL, D_FF)),
        "w_down": w(ks[13], (DEPTH, D_FF, D_MODEL)),
        "final_norm_g": g(ks[14], (D_MODEL,)),
    }


def reference(x_prompt, x_sample, attn_norm_g, w_in, q_a_norm_g, w_q_b, kv_a_norm_g, w_kv_b,
              mla_out_norm_g, dil_out_norm_g, w_o, mlp_norm_g, w_up, w_down, final_norm_g):
    y_prompt = trunk(x_prompt, attn_norm_g, w_in, q_a_norm_g, w_q_b, kv_a_norm_g, w_kv_b,
                     mla_out_norm_g, dil_out_norm_g, w_o, mlp_norm_g, w_up, w_down, final_norm_g)
    y_sample = trunk(x_sample, attn_norm_g, w_in, q_a_norm_g, w_q_b, kv_a_norm_g, w_kv_b,
                     mla_out_norm_g, dil_out_norm_g, w_o, mlp_norm_g, w_up, w_down, final_norm_g)
    return (y_prompt, y_sample)
```

```python
import functools
import math

import jax
import jax.numpy as jnp
from jax import lax
from jax.experimental import pallas as pl
from jax.experimental.pallas import tpu as pltpu

D_MODEL = 2048
EPS = 1e-6
ROPE_THETA = 10000.0
MLA_HEADS = 8
Q_LORA = 512
KV_LORA = 512
NOPE_DIM = 128
ROPE_DIM = 64
V_DIM = 128
MLA_WIDTH = MLA_HEADS * V_DIM
DIL_HEADS = 8
DIL_HD = 128
DIL_WIDTH = DIL_HEADS * DIL_HD
DIL_PATTERNS = ((128, 1), (512, 4), (2048, 16))
D_FF = 4 * D_MODEL

LANES = 128
VMEM_LIMIT_BYTES = 56 * 1024 * 1024

MLA_QK = NOPE_DIM + LANES
ROPE_HALF = ROPE_DIM // 2
MASK_VALUE = -1e30

_NT = (((1,), (1,)), ((), ()))


def _rms(x, g):
    return x * lax.rsqrt(jnp.mean(x * x, axis=-1, keepdims=True) + EPS) * g


def _rope128(x, cos, sin):
    return x * cos + pltpu.roll(x, LANES // 2, 1) * sin


def _bf16_dot(a, b):
    return jnp.dot(a, b, preferred_element_type=jnp.float32)


def _params(*sem):
    return pltpu.CompilerParams(dimension_semantics=sem, vmem_limit_bytes=VMEM_LIMIT_BYTES)


def _mla_pre_kernel(x_ref, g_ref, wm_ref, qg_ref, kvg_ref, wq_ref, wk_ref, wv_ref,
                    cos_ref, sin_ref, q_out, k_out, v_out):
    h = _rms(x_ref[...], g_ref[...]).astype(jnp.bfloat16)
    a = _bf16_dot(h, wm_ref[...])
    qn = _rms(a[:, :Q_LORA], qg_ref[...]).astype(jnp.bfloat16)
    kvn = _rms(a[:, Q_LORA:Q_LORA + KV_LORA], kvg_ref[...]).astype(jnp.bfloat16)
    cos, sin = cos_ref[...], sin_ref[...]
    k_rope = _rope128(a[:, Q_LORA + KV_LORA:], cos, sin).astype(jnp.bfloat16)
    q = _bf16_dot(qn, wq_ref[...])
    k_nope = _bf16_dot(kvn, wk_ref[...])
    for hd in range(MLA_HEADS):
        lo = hd * MLA_QK
        q_out[:, lo:lo + NOPE_DIM] = q[:, lo:lo + NOPE_DIM].astype(jnp.bfloat16)
        q_out[:, lo + NOPE_DIM:lo + MLA_QK] = _rope128(
            q[:, lo + NOPE_DIM:lo + MLA_QK], cos, sin).astype(jnp.bfloat16)
        k_out[:, lo:lo + NOPE_DIM] = k_nope[:, hd * NOPE_DIM:(hd + 1) * NOPE_DIM].astype(jnp.bfloat16)
        k_out[:, lo + NOPE_DIM:lo + MLA_QK] = k_rope
    v_out[...] = _bf16_dot(kvn, wv_ref[...]).astype(jnp.bfloat16)


def _mla_pre(x2d, seq, g, wm, qg, kvg, wq, wk, wv, cos_m, sin_m, tm=256):
    t = x2d.shape[0]
    pos_blocks = seq // tm
    const = lambda i: (0, 0)
    row = lambda i: (i, 0)
    pos = lambda i: (i % pos_blocks, 0)
    return pl.pallas_call(
        _mla_pre_kernel,
        grid=(t // tm,),
        in_specs=[
            pl.BlockSpec((tm, D_MODEL), row),
            pl.BlockSpec((1, D_MODEL), const),
            pl.BlockSpec(wm.shape, const),
            pl.BlockSpec((1, Q_LORA), const),
            pl.BlockSpec((1, KV_LORA), const),
            pl.BlockSpec(wq.shape, const),
            pl.BlockSpec(wk.shape, const),
            pl.BlockSpec(wv.shape, const),
            pl.BlockSpec((tm, LANES), pos),
            pl.BlockSpec((tm, LANES), pos),
        ],
        out_specs=[
            pl.BlockSpec((tm, MLA_HEADS * MLA_QK), row),
            pl.BlockSpec((tm, MLA_HEADS * MLA_QK), row),
            pl.BlockSpec((tm, MLA_WIDTH), row),
        ],
        out_shape=[
            jax.ShapeDtypeStruct((t, MLA_HEADS * MLA_QK), jnp.bfloat16),
            jax.ShapeDtypeStruct((t, MLA_HEADS * MLA_QK), jnp.bfloat16),
            jax.ShapeDtypeStruct((t, MLA_WIDTH), jnp.bfloat16),
        ],
        compiler_params=_params("parallel"),
        name="mla_pre",
    )(x2d, g, wm, qg, kvg, wq, wk, wv, cos_m, sin_m)


def _dil_proj_kernel(x_ref, g_ref, w_ref, cos_ref, sin_ref, o_ref, h_scr):
    j = pl.program_id(1)

    @pl.when(j == 0)
    def _():
        h_scr[...] = _rms(x_ref[...], g_ref[...]).astype(jnp.bfloat16)

    y = _bf16_dot(h_scr[...], w_ref[...])

    @pl.when(j < 2)
    def _():
        cos, sin = cos_ref[...], sin_ref[...]
        for hd in range(DIL_HEADS):
            sl = slice(hd * DIL_HD, (hd + 1) * DIL_HD)
            o_ref[0, :, sl] = _rope128(y[:, sl], cos, sin).astype(jnp.bfloat16)

    @pl.when(j == 2)
    def _():
        o_ref[0] = y.astype(jnp.bfloat16)


def _dil_proj(x2d, seq, g, w_dil, cos_d, sin_d, tm=512):
    t = x2d.shape[0]
    pos_blocks = seq // tm
    return pl.pallas_call(
        _dil_proj_kernel,
        grid=(t // tm, 3),
        in_specs=[
            pl.BlockSpec((tm, D_MODEL), lambda i, j: (i, 0)),
            pl.BlockSpec((1, D_MODEL), lambda i, j: (0, 0)),
            pl.BlockSpec((D_MODEL, DIL_WIDTH), lambda i, j: (0, j)),
            pl.BlockSpec((tm, LANES), lambda i, j: (i % pos_blocks, 0)),
            pl.BlockSpec((tm, LANES), lambda i, j: (i % pos_blocks, 0)),
        ],
        out_specs=pl.BlockSpec((1, tm, DIL_WIDTH), lambda i, j: (j, i, 0)),
        out_shape=jax.ShapeDtypeStruct((3, t, DIL_WIDTH), jnp.bfloat16),
        scratch_shapes=[pltpu.VMEM((tm, D_MODEL), jnp.bfloat16)],
        compiler_params=_params("parallel", "arbitrary"),
        name="dil_proj",
    )(x2d, g, w_dil, cos_d, sin_d)


def _mla_attn_kernel(q_ref, k_ref, v_ref, o_ref):
    scale = 1.0 / math.sqrt(NOPE_DIM + ROPE_DIM)
    s = lax.dot_general(q_ref[0], k_ref[0], _NT, preferred_element_type=jnp.float32) * scale
    m = jnp.max(s, axis=-1, keepdims=True)
    p = jnp.exp(s - m)
    den = jnp.sum(p, axis=-1, keepdims=True)
    o = _bf16_dot(p.astype(jnp.bfloat16), v_ref[0])
    o_ref[0] = o / den


def _mla_attn(q, k, v, tq=256):
    b, s, _ = q.shape
    return pl.pallas_call(
        _mla_attn_kernel,
        grid=(b, MLA_HEADS, s // tq),
        in_specs=[
            pl.BlockSpec((1, tq, MLA_QK), lambda bi, h, i: (bi, i, h)),
            pl.BlockSpec((1, s, MLA_QK), lambda bi, h, i: (bi, 0, h)),
            pl.BlockSpec((1, s, V_DIM), lambda bi, h, i: (bi, 0, h)),
        ],
        out_specs=pl.BlockSpec((1, tq, V_DIM), lambda bi, h, i: (bi, i, h)),
        out_shape=jax.ShapeDtypeStruct((b, s, MLA_WIDTH), jnp.float32),
        compiler_params=_params("parallel", "parallel", "arbitrary"),
        name="mla_attn",
    )(q, k, v)


def _dil_attn_kernel(q_ref, kp_ref, kc_ref, kn_ref, vp_ref, vc_ref, vn_ref, o_ref, lse_ref,
                     *, tq, half, class_len):
    i = pl.program_id(2)
    nk = tq + 2 * half
    q = q_ref[0]
    k = jnp.concatenate([kp_ref[0], kc_ref[0], kn_ref[0]], axis=0)
    v = jnp.concatenate([vp_ref[0], vc_ref[0], vn_ref[0]], axis=0)
    q_pos = i * tq + lax.broadcasted_iota(jnp.int32, (tq, nk), 0)
    k_pos = i * tq - half + lax.broadcasted_iota(jnp.int32, (tq, nk), 1)
    rel = k_pos - q_pos
    valid = jnp.where(rel <= half, jnp.where(rel >= -half, 1, 0), 0)
    valid = jnp.where(k_pos >= 0, jnp.where(k_pos < class_len, valid, 0), 0)
    bias = jnp.where(valid > 0, 0.0, MASK_VALUE).astype(jnp.float32)
    inv_sqrt = 1.0 / math.sqrt(DIL_HD)
    lane = lax.broadcasted_iota(jnp.int32, (tq, LANES), 1)
    lse_tile = jnp.zeros((tq, LANES), jnp.float32)
    for hd in range(DIL_HEADS):
        sl = slice(hd * DIL_HD, (hd + 1) * DIL_HD)
        s = lax.dot_general(q[:, sl], k[:, sl], _NT, preferred_element_type=jnp.float32)
        s = s * inv_sqrt + bias
        m = jnp.max(s, axis=-1, keepdims=True)
        e = jnp.exp(s - m)
        den = jnp.sum(e, axis=-1, keepdims=True)
        o_ref[0, :, sl] = _bf16_dot(e.astype(jnp.bfloat16), v[:, sl]) / den
        lse_tile = jnp.where(lane == hd, m + jnp.log(den), lse_tile)
    lse_ref[0] = lse_tile


def _dil_attn(qkv, batch, seq, window, dilation, tq=128):
    half = window // (2 * dilation)
    class_len = seq // dilation
    tq = min(tq, class_len)
    nb = tq // half
    last = class_len // half - 1
    q, k, v = (qkv[n].reshape(batch, class_len, dilation * DIL_WIDTH) for n in range(3))
    cur = lambda b, r, i: (b, i, r)
    prev = lambda b, r, i: (b, jnp.maximum(i * nb - 1, 0), r)
    nxt = lambda b, r, i: (b, jnp.minimum((i + 1) * nb, last), r)
    o, lse = pl.pallas_call(
        functools.partial(_dil_attn_kernel, tq=tq, half=half, class_len=class_len),
        grid=(batch, dilation, class_len // tq),
        in_specs=[
            pl.BlockSpec((1, tq, DIL_WIDTH), cur),
            pl.BlockSpec((1, half, DIL_WIDTH), prev),
            pl.BlockSpec((1, tq, DIL_WIDTH), cur),
            pl.BlockSpec((1, half, DIL_WIDTH), nxt),
            pl.BlockSpec((1, half, DIL_WIDTH), prev),
            pl.BlockSpec((1, tq, DIL_WIDTH), cur),
            pl.BlockSpec((1, half, DIL_WIDTH), nxt),
        ],
        out_specs=[
            pl.BlockSpec((1, tq, DIL_WIDTH), cur),
            pl.BlockSpec((1, tq, LANES), cur),
        ],
        out_shape=[
            jax.ShapeDtypeStruct((batch, class_len, dilation * DIL_WIDTH), jnp.float32),
            jax.ShapeDtypeStruct((batch, class_len, dilation * LANES), jnp.float32),
        ],
        compiler_params=_params("parallel", "parallel", "arbitrary"),
        name=f"dil_attn_d{dilation}",
    )(q, k, k, k, v, v, v)
    return o.reshape(batch * seq, DIL_WIDTH), lse.reshape(batch * seq, LANES)


def _mix_out_kernel(x_ref, oa_ref, o1_ref, o2_ref, o3_ref, l1_ref, l2_ref, l3_ref,
                    ga_ref, gb_ref, wo_ref, y_ref):
    l1, l2, l3 = l1_ref[...], l2_ref[...], l3_ref[...]
    mx = jnp.maximum(jnp.maximum(l1, l2), l3)
    e1, e2, e3 = jnp.exp(l1 - mx), jnp.exp(l2 - mx), jnp.exp(l3 - mx)
    tot = e1 + e2 + e3
    w1, w2, w3 = e1 / tot, e2 / tot, e3 / tot
    parts = []
    for hd in range(DIL_HEADS):
        sl = slice(hd * DIL_HD, (hd + 1) * DIL_HD)
        col = slice(hd, hd + 1)
        parts.append(w1[:, col] * o1_ref[:, sl] + w2[:, col] * o2_ref[:, sl] + w3[:, col] * o3_ref[:, sl])
    o_b = jnp.concatenate(parts, axis=-1)
    mix_a = _rms(oa_ref[...], ga_ref[...]).astype(jnp.bfloat16)
    mix_b = _rms(o_b, gb_ref[...]).astype(jnp.bfloat16)
    y = _bf16_dot(mix_a, wo_ref[:MLA_WIDTH, :]) + _bf16_dot(mix_b, wo_ref[MLA_WIDTH:, :])
    y_ref[...] = x_ref[...] + y


def _mix_out(x2d, o_a, dil, ga, gb, wo, tm=256):
    t = x2d.shape[0]
    row = lambda i: (i, 0)
    const = lambda i: (0, 0)
    (o1, l1), (o2, l2), (o3, l3) = dil
    return pl.pallas_call(
        _mix_out_kernel,
        grid=(t // tm,),
        in_specs=[
            pl.BlockSpec((tm, D_MODEL), row),
            pl.BlockSpec((tm, MLA_WIDTH), row),
            pl.BlockSpec((tm, DIL_WIDTH), row),
            pl.BlockSpec((tm, DIL_WIDTH), row),
            pl.BlockSpec((tm, DIL_WIDTH), row),
            pl.BlockSpec((tm, LANES), row),
            pl.BlockSpec((tm, LANES), row),
            pl.BlockSpec((tm, LANES), row),
            pl.BlockSpec((1, MLA_WIDTH), const),
            pl.BlockSpec((1, DIL_WIDTH), const),
            pl.BlockSpec(wo.shape, const),
        ],
        out_specs=pl.BlockSpec((tm, D_MODEL), row),
        out_shape=jax.ShapeDtypeStruct((t, D_MODEL), jnp.float32),
        compiler_params=_params("parallel"),
        name="mix_out",
    )(x2d, o_a, o1, o2, o3, l1, l2, l3, ga, gb, wo)


def _mlp_kernel(x_ref, g_ref, wu_ref, wd_ref, gf_ref, y_ref, h_scr, acc_scr):
    k = pl.program_id(1)

    @pl.when(k == 0)
    def _():
        x = x_ref[...]
        h_scr[...] = _rms(x, g_ref[...]).astype(jnp.bfloat16)
        acc_scr[...] = x

    u = _bf16_dot(h_scr[...], wu_ref[...])
    a = jnp.square(jnp.maximum(u, 0.0)).astype(jnp.bfloat16)
    acc_scr[...] += _bf16_dot(a, wd_ref[...])

    @pl.when(k == pl.num_programs(1) - 1)
    def _():
        y_ref[...] = _rms(acc_scr[...], gf_ref[...])


def _mlp(x2d, g, wu, wd, gf, tm=512, tf=1024):
    t = x2d.shape[0]
    return pl.pallas_call(
        _mlp_kernel,
        grid=(t // tm, D_FF // tf),
        in_specs=[
            pl.BlockSpec((tm, D_MODEL), lambda i, k: (i, 0)),
            pl.BlockSpec((1, D_MODEL), lambda i, k: (0, 0)),
            pl.BlockSpec((D_MODEL, tf), lambda i, k: (0, k)),
            pl.BlockSpec((tf, D_MODEL), lambda i, k: (k, 0)),
            pl.BlockSpec((1, D_MODEL), lambda i, k: (0, 0)),
        ],
        out_specs=pl.BlockSpec((tm, D_MODEL), lambda i, k: (i, 0)),
        out_shape=jax.ShapeDtypeStruct((t, D_MODEL), jnp.float32),
        scratch_shapes=[
            pltpu.VMEM((tm, D_MODEL), jnp.bfloat16),
            pltpu.VMEM((tm, D_MODEL), jnp.float32),
        ],
        compiler_params=_params("parallel", "arbitrary"),
        name="mlp",
    )(x2d, g, wu, wd, gf)


def _rope_tables(seq):
    pos = jnp.arange(seq, dtype=jnp.float32)

    def cos_sin(dim):
        inv = 1.0 / (ROPE_THETA ** (jnp.arange(0, dim, 2, dtype=jnp.float32) / dim))
        ang = pos[:, None] * inv[None, :]
        return jnp.cos(ang), jnp.sin(ang)

    c, s = cos_sin(DIL_HD)
    cos_d = jnp.concatenate([c, c], axis=-1)
    sin_d = jnp.concatenate([-s, s], axis=-1)
    c, s = cos_sin(ROPE_DIM)
    z = jnp.zeros_like(c)
    cos_m = jnp.concatenate([c, z, c, z], axis=-1)
    sin_m = jnp.concatenate([-s, z, s, z], axis=-1)
    return cos_d, sin_d, cos_m, sin_m


def _spread_rope_cols(w):
    z = jnp.zeros(w.shape[:-1] + (ROPE_HALF,), w.dtype)
    return jnp.concatenate([w[..., :ROPE_HALF], z, w[..., ROPE_HALF:], z], axis=-1)


def _prepare_weights(w_in, w_q_b, w_kv_b, w_o, w_up, w_down):
    bf = jnp.bfloat16
    n_lora = Q_LORA + KV_LORA
    wm = jnp.concatenate([w_in[:, :n_lora], _spread_rope_cols(w_in[:, n_lora:n_lora + ROPE_DIM])],
                         axis=-1).astype(bf)
    w_dil = w_in[:, n_lora + ROPE_DIM:].astype(bf)
    wq = w_q_b.reshape(Q_LORA, MLA_HEADS, NOPE_DIM + ROPE_DIM)
    wq = jnp.concatenate([wq[..., :NOPE_DIM], _spread_rope_cols(wq[..., NOPE_DIM:])], axis=-1)
    wq = wq.reshape(Q_LORA, MLA_HEADS * MLA_QK).astype(bf)
    wkv = w_kv_b.reshape(KV_LORA, MLA_HEADS, NOPE_DIM + V_DIM)
    wk = wkv[..., :NOPE_DIM].reshape(KV_LORA, MLA_HEADS * NOPE_DIM).astype(bf)
    wv = wkv[..., NOPE_DIM:].reshape(KV_LORA, MLA_WIDTH).astype(bf)
    return wm, w_dil, wq, wk, wv, w_o.astype(bf), w_up.astype(bf), w_down.astype(bf)


def _trunk(x, gains, weights):
    attn_g, qa_g, kva_g, mla_g, dil_g, mlp_g, final_g = gains
    wm, w_dil, wq, wk, wv, wo, wu, wd = weights
    batch, seq, _ = x.shape
    x2d = x.reshape(batch * seq, D_MODEL)
    cos_d, sin_d, cos_m, sin_m = _rope_tables(seq)

    q, k, v = _mla_pre(x2d, seq, attn_g, wm, qa_g, kva_g, wq, wk, wv, cos_m, sin_m)
    qkv_d = _dil_proj(x2d, seq, attn_g, w_dil, cos_d, sin_d)
    o_a = _mla_attn(q.reshape(batch, seq, -1), k.reshape(batch, seq, -1), v.reshape(batch, seq, -1))
    dil = [_dil_attn(qkv_d, batch, seq, window, dilation) for window, dilation in DIL_PATTERNS]
    x1 = _mix_out(x2d, o_a.reshape(batch * seq, MLA_WIDTH), dil, mla_g, dil_g, wo)
    y = _mlp(x1, mlp_g, wu, wd, final_g)
    return y.reshape(batch, seq, D_MODEL)


def kernel(x_prompt, x_sample, attn_norm_g, w_in, q_a_norm_g, w_q_b, kv_a_norm_g, w_kv_b,
           mla_out_norm_g, dil_out_norm_g, w_o, mlp_norm_g, w_up, w_down, final_norm_g):
    assert w_in.shape[0] == 1, "single-layer block"
    weights = _prepare_weights(w_in[0], w_q_b[0], w_kv_b[0], w_o[0], w_up[0], w_down[0])
    gains = (attn_norm_g[0][None], q_a_norm_g[0][None], kv_a_norm_g[0][None],
             mla_out_norm_g[0][None], dil_out_norm_g[0][None], mlp_norm_g[0][None],
             final_norm_g[None])
    return (_trunk(x_prompt, gains, weights), _trunk(x_sample, gains, weights))
```

```python
import functools
import math

import jax
import jax.numpy as jnp
from jax import lax
from jax.experimental import pallas as pl
from jax.experimental.pallas import tpu as pltpu

D_MODEL = 2048
EPS = 1e-6
ROPE_THETA = 10000.0
MLA_HEADS = 8
Q_LORA = 512
KV_LORA = 512
NOPE_DIM = 128
ROPE_DIM = 64
V_DIM = 128
MLA_WIDTH = MLA_HEADS * V_DIM
DIL_HEADS = 8
DIL_HD = 128
DIL_WIDTH = DIL_HEADS * DIL_HD
DIL_PATTERNS = ((128, 1), (512, 4), (2048, 16))
D_FF = 4 * D_MODEL

LANES = 128
VMEM_LIMIT_BYTES = 56 * 1024 * 1024

MLA_QK = NOPE_DIM + LANES
ROPE_HALF = ROPE_DIM // 2
MASK_VALUE = -1e30

_NT = (((1,), (1,)), ((), ()))


def _rms(x, g):
    return x * lax.rsqrt(jnp.mean(x * x, axis=-1, keepdims=True) + EPS) * g


def _rope128(x, cos, sin):
    return x * cos + pltpu.roll(x, LANES // 2, 1) * sin


def _bf16_dot(a, b):
    return jnp.dot(a, b, preferred_element_type=jnp.float32)


def _params(*sem):
    return pltpu.CompilerParams(dimension_semantics=sem, vmem_limit_bytes=VMEM_LIMIT_BYTES)


def _mla_pre_kernel(x_ref, g_ref, wm_ref, qg_ref, kvg_ref, wq_ref, wk_ref, wv_ref,
                    cos_ref, sin_ref, q_out, k_out, v_out):
    h = _rms(x_ref[...], g_ref[...]).astype(jnp.bfloat16)
    a = _bf16_dot(h, wm_ref[...])
    qn = _rms(a[:, :Q_LORA], qg_ref[...]).astype(jnp.bfloat16)
    kvn = _rms(a[:, Q_LORA:Q_LORA + KV_LORA], kvg_ref[...]).astype(jnp.bfloat16)
    cos, sin = cos_ref[...], sin_ref[...]
    k_rope = _rope128(a[:, Q_LORA + KV_LORA:], cos, sin).astype(jnp.bfloat16)
    q = _bf16_dot(qn, wq_ref[...])
    k_nope = _bf16_dot(kvn, wk_ref[...])
    for hd in range(MLA_HEADS):
        lo = hd * MLA_QK
        q_out[:, lo:lo + NOPE_DIM] = q[:, lo:lo + NOPE_DIM].astype(jnp.bfloat16)
        q_out[:, lo + NOPE_DIM:lo + MLA_QK] = _rope128(
            q[:, lo + NOPE_DIM:lo + MLA_QK], cos, sin).astype(jnp.bfloat16)
        k_out[:, lo:lo + NOPE_DIM] = k_nope[:, hd * NOPE_DIM:(hd + 1) * NOPE_DIM].astype(jnp.bfloat16)
        k_out[:, lo + NOPE_DIM:lo + MLA_QK] = k_rope
    v_out[...] = _bf16_dot(kvn, wv_ref[...]).astype(jnp.bfloat16)


def _mla_pre(x2d, seq, g, wm, qg, kvg, wq, wk, wv, cos_m, sin_m, tm=256):
    t = x2d.shape[0]
    pos_blocks = seq // tm
    const = lambda i: (0, 0)
    row = lambda i: (i, 0)
    pos = lambda i: (i % pos_blocks, 0)
    return pl.pallas_call(
        _mla_pre_kernel,
        grid=(t // tm,),
        in_specs=[
            pl.BlockSpec((tm, D_MODEL), row),
            pl.BlockSpec((1, D_MODEL), const),
            pl.BlockSpec(wm.shape, const),
            pl.BlockSpec((1, Q_LORA), const),
            pl.BlockSpec((1, KV_LORA), const),
            pl.BlockSpec(wq.shape, const),
            pl.BlockSpec(wk.shape, const),
            pl.BlockSpec(wv.shape, const),
            pl.BlockSpec((tm, LANES), pos),
            pl.BlockSpec((tm, LANES), pos),
        ],
        out_specs=[
            pl.BlockSpec((tm, MLA_HEADS * MLA_QK), row),
            pl.BlockSpec((tm, MLA_HEADS * MLA_QK), row),
            pl.BlockSpec((tm, MLA_WIDTH), row),
        ],
        out_shape=[
            jax.ShapeDtypeStruct((t, MLA_HEADS * MLA_QK), jnp.bfloat16),
            jax.ShapeDtypeStruct((t, MLA_HEADS * MLA_QK), jnp.bfloat16),
            jax.ShapeDtypeStruct((t, MLA_WIDTH), jnp.bfloat16),
        ],
        compiler_params=_params("parallel"),
        name="mla_pre",
    )(x2d, g, wm, qg, kvg, wq, wk, wv, cos_m, sin_m)


def _dil_proj_kernel(x_ref, g_ref, w_ref, cos_ref, sin_ref, *refs):
    out_refs, (h_scr, y_scr) = refs[:len(DIL_PATTERNS)], refs[len(DIL_PATTERNS):]
    j = pl.program_id(1)
    tm = y_scr.shape[1]

    @pl.when(j == 0)
    def _():
        h_scr[...] = _rms(x_ref[...], g_ref[...]).astype(jnp.bfloat16)

    y = _bf16_dot(h_scr[...], w_ref[...])

    @pl.when(j < 2)
    def _():
        cos, sin = cos_ref[...], sin_ref[...]
        for hd in range(DIL_HEADS):
            y_scr[hd] = _rope128(y[:, hd * DIL_HD:(hd + 1) * DIL_HD], cos, sin)

    @pl.when(j == 2)
    def _():
        for hd in range(DIL_HEADS):
            y_scr[hd] = y[:, hd * DIL_HD:(hd + 1) * DIL_HD]

    for o_ref, (_, dilation) in zip(out_refs, DIL_PATTERNS):
        for r in range(dilation):
            rows = pl.ds(r, tm // dilation, stride=dilation) if dilation > 1 else slice(None)
            for hd in range(DIL_HEADS):
                o_ref[0, 0, r, :, hd * DIL_HD:(hd + 1) * DIL_HD] = y_scr[hd, rows, :].astype(jnp.bfloat16)


def _dil_proj(x2d, batch, seq, g, w_dil, cos_d, sin_d, tm=512):
    t = x2d.shape[0]
    tiles = seq // tm
    pos = lambda i, j: (i % tiles, 0)
    return pl.pallas_call(
        _dil_proj_kernel,
        grid=(t // tm, 3),
        in_specs=[
            pl.BlockSpec((tm, D_MODEL), lambda i, j: (i, 0)),
            pl.BlockSpec((1, D_MODEL), lambda i, j: (0, 0)),
            pl.BlockSpec((D_MODEL, DIL_WIDTH), lambda i, j: (0, j)),
            pl.BlockSpec((tm, LANES), pos),
            pl.BlockSpec((tm, LANES), pos),
        ],
        out_specs=[
            pl.BlockSpec((1, 1, d, tm // d, DIL_WIDTH), lambda i, j: (j, i // tiles, 0, i % tiles, 0))
            for _, d in DIL_PATTERNS
        ],
        out_shape=[
            jax.ShapeDtypeStruct((3, batch, d, seq // d, DIL_WIDTH), jnp.bfloat16)
            for _, d in DIL_PATTERNS
        ],
        scratch_shapes=[
            pltpu.VMEM((tm, D_MODEL), jnp.bfloat16),
            pltpu.VMEM((DIL_HEADS, tm, DIL_HD), jnp.float32),
        ],
        compiler_params=_params("parallel", "arbitrary"),
        name="dil_proj",
    )(x2d, g, w_dil, cos_d, sin_d)


def _mla_attn_kernel(q_ref, k_ref, v_ref, o_ref, vext_scr, *, chunk):
    exp2_scale = math.log2(math.e) / math.sqrt(NOPE_DIM + ROPE_DIM)

    @pl.when(pl.program_id(2) == 0)
    def _():
        vext_scr[:, :V_DIM] = v_ref[0]
        vext_scr[:, V_DIM:] = jnp.ones((vext_scr.shape[0], V_DIM), jnp.bfloat16)

    q = q_ref[0]
    m = acc = None
    for c in range(k_ref.shape[1] // chunk):
        rows = slice(c * chunk, (c + 1) * chunk)
        s = lax.dot_general(q, k_ref[0, rows, :], _NT, preferred_element_type=jnp.float32)
        m_c = jnp.max(s, axis=-1, keepdims=True)
        m_new = m_c if m is None else jnp.maximum(m, m_c)
        p = jnp.exp2((s - m_new) * exp2_scale).astype(jnp.bfloat16)
        pv = _bf16_dot(p, vext_scr[rows, :])
        acc = pv if m is None else acc * jnp.exp2((m - m_new) * exp2_scale) + pv
        m = m_new
    o_ref[0] = acc[:, :V_DIM] / acc[:, V_DIM:V_DIM + 1]


def _mla_attn(q, k, v, tq=1024, chunk=512):
    b, s, _ = q.shape
    return pl.pallas_call(
        functools.partial(_mla_attn_kernel, chunk=chunk),
        grid=(b, MLA_HEADS, s // tq),
        in_specs=[
            pl.BlockSpec((1, tq, MLA_QK), lambda bi, h, i: (bi, i, h)),
            pl.BlockSpec((1, s, MLA_QK), lambda bi, h, i: (bi, 0, h)),
            pl.BlockSpec((1, s, V_DIM), lambda bi, h, i: (bi, 0, h)),
        ],
        out_specs=pl.BlockSpec((1, tq, V_DIM), lambda bi, h, i: (bi, i, h)),
        out_shape=jax.ShapeDtypeStruct((b, s, MLA_WIDTH), jnp.float32),
        scratch_shapes=[pltpu.VMEM((s, 2 * V_DIM), jnp.bfloat16)],
        compiler_params=_params("parallel", "parallel", "arbitrary"),
        name="mla_attn",
    )(q, k, v)


def _dil_attn_kernel(q_ref, kp_ref, kc_ref, kn_ref, vp_ref, vc_ref, vn_ref, o_ref, lse_ref, o_scr,
                     *, tq, half, class_len, dilation):
    i = pl.program_id(1)
    r = pl.program_id(2)
    nk = tq + 2 * half
    q = q_ref[0, 0, 0]
    k = jnp.concatenate([kp_ref[0, 0, 0], kc_ref[0, 0, 0], kn_ref[0, 0, 0]], axis=0)
    v = jnp.concatenate([vp_ref[0, 0, 0], vc_ref[0, 0, 0], vn_ref[0, 0, 0]], axis=0)
    q_pos = i * tq + lax.broadcasted_iota(jnp.int32, (tq, nk), 0)
    k_pos = i * tq - half + lax.broadcasted_iota(jnp.int32, (tq, nk), 1)
    rel = k_pos - q_pos
    valid = jnp.where(rel <= half, jnp.where(rel >= -half, 1, 0), 0)
    valid = jnp.where(k_pos >= 0, jnp.where(k_pos < class_len, valid, 0), 0)
    bias = jnp.where(valid > 0, 0.0, MASK_VALUE).astype(jnp.float32)
    inv_sqrt = 1.0 / math.sqrt(DIL_HD)
    lane = lax.broadcasted_iota(jnp.int32, (tq, LANES), 1)
    lse_tile = jnp.zeros((tq, LANES), jnp.float32)
    for hd in range(DIL_HEADS):
        sl = slice(hd * DIL_HD, (hd + 1) * DIL_HD)
        s = lax.dot_general(q[:, sl], k[:, sl], _NT, preferred_element_type=jnp.float32)
        s = s * inv_sqrt + bias
        m = jnp.max(s, axis=-1, keepdims=True)
        e = jnp.exp(s - m)
        den = jnp.sum(e, axis=-1, keepdims=True)
        o_scr[hd] = _bf16_dot(e.astype(jnp.bfloat16), v[:, sl]) / den
        lse_tile = jnp.where(lane == hd, m + jnp.log(den), lse_tile)

    for rs in range(dilation):
        @pl.when(r == rs)
        def _():
            rows = pl.ds(rs, tq, stride=dilation) if dilation > 1 else slice(None)
            for hd in range(DIL_HEADS):
                o_ref[0, hd, rows, :] = o_scr[hd]
            lse_ref[0, rows, :] = lse_tile


def _dil_attn(qkv, window, tq=128):
    _, batch, dilation, class_len, _ = qkv.shape
    half = window // (2 * dilation)
    tq = min(tq, class_len)
    nb = tq // half
    last = class_len // half - 1

    def spec(which, rows, row_block):
        return pl.BlockSpec((1, 1, 1, rows, DIL_WIDTH), lambda b, i, r: (which, b, r, row_block(i), 0))

    cur = lambda i: i
    prev = lambda i: jnp.maximum(i * nb - 1, 0)
    nxt = lambda i: jnp.minimum((i + 1) * nb, last)
    return pl.pallas_call(
        functools.partial(_dil_attn_kernel, tq=tq, half=half, class_len=class_len, dilation=dilation),
        grid=(batch, class_len // tq, dilation),
        in_specs=[
            spec(0, tq, cur),
            spec(1, half, prev), spec(1, tq, cur), spec(1, half, nxt),
            spec(2, half, prev), spec(2, tq, cur), spec(2, half, nxt),
        ],
        out_specs=[
            pl.BlockSpec((1, DIL_HEADS, tq * dilation, DIL_HD), lambda b, i, r: (b, 0, i, 0)),
            pl.BlockSpec((1, tq * dilation, LANES), lambda b, i, r: (b, i, 0)),
        ],
        out_shape=[
            jax.ShapeDtypeStruct((batch, DIL_HEADS, class_len * dilation, DIL_HD), jnp.float32),
            jax.ShapeDtypeStruct((batch, class_len * dilation, LANES), jnp.float32),
        ],
        scratch_shapes=[pltpu.VMEM((DIL_HEADS, tq, DIL_HD), jnp.float32)],
        compiler_params=_params("parallel", "parallel", "arbitrary"),
        name=f"dil_attn_d{dilation}",
    )(qkv, qkv, qkv, qkv, qkv, qkv, qkv)


def _mix_out_kernel(x_ref, oa_ref, o1_ref, o2_ref, o3_ref, l1_ref, l2_ref, l3_ref,
                    ga_ref, gb_ref, wo_ref, y_ref):
    l1, l2, l3 = l1_ref[...], l2_ref[...], l3_ref[...]
    mx = jnp.maximum(jnp.maximum(l1, l2), l3)
    e1, e2, e3 = jnp.exp(l1 - mx), jnp.exp(l2 - mx), jnp.exp(l3 - mx)
    tot = e1 + e2 + e3
    w1, w2, w3 = e1 / tot, e2 / tot, e3 / tot
    parts = []
    for hd in range(DIL_HEADS):
        col = slice(hd, hd + 1)
        parts.append(w1[:, col] * o1_ref[0, hd] + w2[:, col] * o2_ref[0, hd] + w3[:, col] * o3_ref[0, hd])
    o_b = jnp.concatenate(parts, axis=-1)
    mix_a = _rms(oa_ref[...], ga_ref[...]).astype(jnp.bfloat16)
    mix_b = _rms(o_b, gb_ref[...]).astype(jnp.bfloat16)
    y = _bf16_dot(mix_a, wo_ref[:MLA_WIDTH, :]) + _bf16_dot(mix_b, wo_ref[MLA_WIDTH:, :])
    y_ref[...] = x_ref[...] + y


def _mix_out(x2d, seq, o_a, dil, ga, gb, wo, tm=256):
    t = x2d.shape[0]
    tiles = seq // tm
    row = lambda i: (i, 0)
    const = lambda i: (0, 0)
    heads = pl.BlockSpec((1, DIL_HEADS, tm, DIL_HD), lambda i: (i // tiles, 0, i % tiles, 0))
    (o1, l1), (o2, l2), (o3, l3) = dil
    return pl.pallas_call(
        _mix_out_kernel,
        grid=(t // tm,),
        in_specs=[
            pl.BlockSpec((tm, D_MODEL), row),
            pl.BlockSpec((tm, MLA_WIDTH), row),
            heads,
            heads,
            heads,
            pl.BlockSpec((tm, LANES), row),
            pl.BlockSpec((tm, LANES), row),
            pl.BlockSpec((tm, LANES), row),
            pl.BlockSpec((1, MLA_WIDTH), const),
            pl.BlockSpec((1, DIL_WIDTH), const),
            pl.BlockSpec(wo.shape, const),
        ],
        out_specs=pl.BlockSpec((tm, D_MODEL), row),
        out_shape=jax.ShapeDtypeStruct((t, D_MODEL), jnp.float32),
        compiler_params=_params("parallel"),
        name="mix_out",
    )(x2d, o_a, o1, o2, o3, l1, l2, l3, ga, gb, wo)


def _mlp_kernel(x_ref, g_ref, wu_ref, wd_ref, gf_ref, y_ref, h_scr, acc_scr):
    k = pl.program_id(1)

    @pl.when(k == 0)
    def _():
        x = x_ref[...]
        h_scr[...] = _rms(x, g_ref[...]).astype(jnp.bfloat16)
        acc_scr[...] = x

    u = _bf16_dot(h_scr[...], wu_ref[...])
    a = jnp.square(jnp.maximum(u, 0.0)).astype(jnp.bfloat16)
    acc_scr[...] += _bf16_dot(a, wd_ref[...])

    @pl.when(k == pl.num_programs(1) - 1)
    def _():
        y_ref[...] = _rms(acc_scr[...], gf_ref[...])


def _mlp(x2d, g, wu, wd, gf, tm=512, tf=1024):
    t = x2d.shape[0]
    return pl.pallas_call(
        _mlp_kernel,
        grid=(t // tm, D_FF // tf),
        in_specs=[
            pl.BlockSpec((tm, D_MODEL), lambda i, k: (i, 0)),
            pl.BlockSpec((1, D_MODEL), lambda i, k: (0, 0)),
            pl.BlockSpec((D_MODEL, tf), lambda i, k: (0, k)),
            pl.BlockSpec((tf, D_MODEL), lambda i, k: (k, 0)),
            pl.BlockSpec((1, D_MODEL), lambda i, k: (0, 0)),
        ],
        out_specs=pl.BlockSpec((tm, D_MODEL), lambda i, k: (i, 0)),
        out_shape=jax.ShapeDtypeStruct((t, D_MODEL), jnp.float32),
        scratch_shapes=[
            pltpu.VMEM((tm, D_MODEL), jnp.bfloat16),
            pltpu.VMEM((tm, D_MODEL), jnp.float32),
        ],
        compiler_params=_params("parallel", "arbitrary"),
        name="mlp",
    )(x2d, g, wu, wd, gf)


def _rope_tables(seq):
    pos = jnp.arange(seq, dtype=jnp.float32)

    def cos_sin(dim):
        inv = 1.0 / (ROPE_THETA ** (jnp.arange(0, dim, 2, dtype=jnp.float32) / dim))
        ang = pos[:, None] * inv[None, :]
        return jnp.cos(ang), jnp.sin(ang)

    c, s = cos_sin(DIL_HD)
    cos_d = jnp.concatenate([c, c], axis=-1)
    sin_d = jnp.concatenate([-s, s], axis=-1)
    c, s = cos_sin(ROPE_DIM)
    z = jnp.zeros_like(c)
    cos_m = jnp.concatenate([c, z, c, z], axis=-1)
    sin_m = jnp.concatenate([-s, z, s, z], axis=-1)
    return cos_d, sin_d, cos_m, sin_m


def _spread_rope_cols(w):
    z = jnp.zeros(w.shape[:-1] + (ROPE_HALF,), w.dtype)
    return jnp.concatenate([w[..., :ROPE_HALF], z, w[..., ROPE_HALF:], z], axis=-1)


def _prepare_weights(w_in, w_q_b, w_kv_b, w_o, w_up, w_down):
    bf = jnp.bfloat16
    n_lora = Q_LORA + KV_LORA
    wm = jnp.concatenate([w_in[:, :n_lora], _spread_rope_cols(w_in[:, n_lora:n_lora + ROPE_DIM])],
                         axis=-1).astype(bf)
    w_dil = w_in[:, n_lora + ROPE_DIM:].astype(bf)
    wq = w_q_b.reshape(Q_LORA, MLA_HEADS, NOPE_DIM + ROPE_DIM)
    wq = jnp.concatenate([wq[..., :NOPE_DIM], _spread_rope_cols(wq[..., NOPE_DIM:])], axis=-1)
    wq = wq.reshape(Q_LORA, MLA_HEADS * MLA_QK).astype(bf)
    wkv = w_kv_b.reshape(KV_LORA, MLA_HEADS, NOPE_DIM + V_DIM)
    wk = wkv[..., :NOPE_DIM].reshape(KV_LORA, MLA_HEADS * NOPE_DIM).astype(bf)
    wv = wkv[..., NOPE_DIM:].reshape(KV_LORA, MLA_WIDTH).astype(bf)
    return wm, w_dil, wq, wk, wv, w_o.astype(bf), w_up.astype(bf), w_down.astype(bf)


def _trunk(x, gains, weights):
    attn_g, qa_g, kva_g, mla_g, dil_g, mlp_g, final_g = gains
    wm, w_dil, wq, wk, wv, wo, wu, wd = weights
    batch, seq, _ = x.shape
    x2d = x.reshape(batch * seq, D_MODEL)
    cos_d, sin_d, cos_m, sin_m = _rope_tables(seq)

    q, k, v = _mla_pre(x2d, seq, attn_g, wm, qa_g, kva_g, wq, wk, wv, cos_m, sin_m)
    qkv_d = _dil_proj(x2d, batch, seq, attn_g, w_dil, cos_d, sin_d)
    o_a = _mla_attn(q.reshape(batch, seq, -1), k.reshape(batch, seq, -1), v.reshape(batch, seq, -1))
    dil = []
    for qkv, (window, _) in zip(qkv_d, DIL_PATTERNS):
        o, lse = _dil_attn(qkv, window)
        dil.append((o, lse.reshape(batch * seq, LANES)))
    x1 = _mix_out(x2d, seq, o_a.reshape(batch * seq, MLA_WIDTH), dil, mla_g, dil_g, wo)
    y = _mlp(x1, mlp_g, wu, wd, final_g)
    return y.reshape(batch, seq, D_MODEL)


def kernel(x_prompt, x_sample, attn_norm_g, w_in, q_a_norm_g, w_q_b, kv_a_norm_g, w_kv_b,
           mla_out_norm_g, dil_out_norm_g, w_o, mlp_norm_g, w_up, w_down, final_norm_g):
    assert w_in.shape[0] == 1, "single-layer block"
    weights = _prepare_weights(w_in[0], w_q_b[0], w_kv_b[0], w_o[0], w_up[0], w_down[0])
    gains = (attn_norm_g[0][None], q_a_norm_g[0][None], kv_a_norm_g[0][None],
             mla_out_norm_g[0][None], dil_out_norm_g[0][None], mlp_norm_g[0][None],
             final_norm_g[None])
    return (_trunk(x_prompt, gains, weights), _trunk(x_sample, gains, weights))
```

```python
import functools
import math

import jax
import jax.numpy as jnp
from jax import lax
from jax.experimental import pallas as pl
from jax.experimental.pallas import tpu as pltpu

D_MODEL = 2048
EPS = 1e-6
ROPE_THETA = 10000.0
MLA_HEADS = 8
Q_LORA = 512
KV_LORA = 512
NOPE_DIM = 128
ROPE_DIM = 64
V_DIM = 128
MLA_WIDTH = MLA_HEADS * V_DIM
DIL_HEADS = 8
DIL_HD = 128
DIL_WIDTH = DIL_HEADS * DIL_HD
DIL_PATTERNS = ((128, 1), (512, 4), (2048, 16))
D_FF = 4 * D_MODEL

LANES = 128
VMEM_LIMIT_BYTES = 56 * 1024 * 1024

MLA_QK = NOPE_DIM + LANES
ROPE_HALF = ROPE_DIM // 2
MASK_VALUE = -1e30

_NT = (((1,), (1,)), ((), ()))


def _rms(x, g):
    return x * lax.rsqrt(jnp.mean(x * x, axis=-1, keepdims=True) + EPS) * g


def _rope128(x, cos, sin):
    return x * cos + pltpu.roll(x, LANES // 2, 1) * sin


def _bf16_dot(a, b):
    return jnp.dot(a, b, preferred_element_type=jnp.float32)


def _params(*sem):
    return pltpu.CompilerParams(dimension_semantics=sem, vmem_limit_bytes=VMEM_LIMIT_BYTES)


def _mla_pre_kernel(x_ref, g_ref, wm_ref, qg_ref, kvg_ref, wq_ref, wk_ref, wv_ref,
                    cos_ref, sin_ref, q_out, k_out, v_out):
    h = _rms(x_ref[...], g_ref[...]).astype(jnp.bfloat16)
    a = _bf16_dot(h, wm_ref[...])
    qn = _rms(a[:, :Q_LORA], qg_ref[...]).astype(jnp.bfloat16)
    kvn = _rms(a[:, Q_LORA:Q_LORA + KV_LORA], kvg_ref[...]).astype(jnp.bfloat16)
    cos, sin = cos_ref[...], sin_ref[...]
    k_rope = _rope128(a[:, Q_LORA + KV_LORA:], cos, sin).astype(jnp.bfloat16)
    q = _bf16_dot(qn, wq_ref[...])
    k_nope = _bf16_dot(kvn, wk_ref[...])
    for hd in range(MLA_HEADS):
        lo = hd * MLA_QK
        q_out[:, lo:lo + NOPE_DIM] = q[:, lo:lo + NOPE_DIM].astype(jnp.bfloat16)
        q_out[:, lo + NOPE_DIM:lo + MLA_QK] = _rope128(
            q[:, lo + NOPE_DIM:lo + MLA_QK], cos, sin).astype(jnp.bfloat16)
        k_out[:, lo:lo + NOPE_DIM] = k_nope[:, hd * NOPE_DIM:(hd + 1) * NOPE_DIM].astype(jnp.bfloat16)
        k_out[:, lo + NOPE_DIM:lo + MLA_QK] = k_rope
    v_out[...] = _bf16_dot(kvn, wv_ref[...]).astype(jnp.bfloat16)


def _resident(shape):
    return pl.BlockSpec(shape, lambda *_: (0,) * len(shape), pipeline_mode=pl.Buffered(1))


def _mla_pre(x2d, seq, g, wm, qg, kvg, wq, wk, wv, cos_m, sin_m, tm=512):
    t = x2d.shape[0]
    pos_blocks = seq // tm
    const = lambda i: (0, 0)
    row = lambda i: (i, 0)
    pos = lambda i: (i % pos_blocks, 0)
    return pl.pallas_call(
        _mla_pre_kernel,
        grid=(t // tm,),
        in_specs=[
            pl.BlockSpec((tm, D_MODEL), row),
            pl.BlockSpec((1, D_MODEL), const),
            _resident(wm.shape),
            pl.BlockSpec((1, Q_LORA), const),
            pl.BlockSpec((1, KV_LORA), const),
            _resident(wq.shape),
            _resident(wk.shape),
            _resident(wv.shape),
            pl.BlockSpec((tm, LANES), pos),
            pl.BlockSpec((tm, LANES), pos),
        ],
        out_specs=[
            pl.BlockSpec((tm, MLA_HEADS * MLA_QK), row),
            pl.BlockSpec((tm, MLA_HEADS * MLA_QK), row),
            pl.BlockSpec((tm, MLA_WIDTH), row),
        ],
        out_shape=[
            jax.ShapeDtypeStruct((t, MLA_HEADS * MLA_QK), jnp.bfloat16),
            jax.ShapeDtypeStruct((t, MLA_HEADS * MLA_QK), jnp.bfloat16),
            jax.ShapeDtypeStruct((t, MLA_WIDTH), jnp.bfloat16),
        ],
        compiler_params=_params("parallel"),
        name="mla_pre",
    )(x2d, g, wm, qg, kvg, wq, wk, wv, cos_m, sin_m)


def _dil_proj_kernel(x_ref, g_ref, w_ref, cos_ref, sin_ref, *refs):
    n = len(DIL_PATTERNS)
    out_refs, planes = refs[:n], refs[n:]
    dils = [d for _, d in DIL_PATTERNS]
    tm = x_ref.shape[0]
    h = _rms(x_ref[...], g_ref[...]).astype(jnp.bfloat16)
    cos, sin = cos_ref[...], sin_ref[...]
    for which in range(3):
        y = _bf16_dot(h, w_ref[:, which * DIL_WIDTH:(which + 1) * DIL_WIDTH])
        for hd in range(DIL_HEADS):
            lanes = slice(hd * DIL_HD, (hd + 1) * DIL_HD)
            yh = y[:, lanes]
            if which < 2:
                yh = _rope128(yh, cos, sin)
            out_refs[0][which, 0, 0, :, lanes] = yh.astype(jnp.bfloat16)
            planes[0][hd] = yh
            for lvl in range(1, n):
                d_prev, d = dils[lvl - 1], dils[lvl]
                ratio, rows_prev, rows = d // d_prev, tm // d_prev, tm // d
                for r_prev in range(d_prev):
                    for sub in range(ratio):
                        c = planes[lvl - 1][hd, pl.ds(r_prev * rows_prev + sub, rows, stride=ratio), :]
                        cls = d_prev * sub + r_prev
                        out_refs[lvl][which, 0, cls, :, lanes] = c.astype(jnp.bfloat16)
                        if lvl + 1 < n:
                            planes[lvl][hd, cls * rows:(cls + 1) * rows, :] = c


def _dil_proj(x2d, batch, seq, g, w_dil, cos_d, sin_d, tm=512):
    t = x2d.shape[0]
    tiles = seq // tm
    dils = [d for _, d in DIL_PATTERNS]
    assert dils[0] == 1 and all(b % a == 0 for a, b in zip(dils, dils[1:]))
    const = lambda i: (0, 0)
    pos = lambda i: (i % tiles, 0)
    return pl.pallas_call(
        _dil_proj_kernel,
        grid=(t // tm,),
        in_specs=[
            pl.BlockSpec((tm, D_MODEL), lambda i: (i, 0)),
            pl.BlockSpec((1, D_MODEL), const),
            _resident(w_dil.shape),
            pl.BlockSpec((tm, LANES), pos),
            pl.BlockSpec((tm, LANES), pos),
        ],
        out_specs=[
            pl.BlockSpec((3, 1, d, tm // d, DIL_WIDTH), lambda i: (0, i // tiles, 0, i % tiles, 0))
            for d in dils
        ],
        out_shape=[jax.ShapeDtypeStruct((3, batch, d, seq // d, DIL_WIDTH), jnp.bfloat16) for d in dils],
        scratch_shapes=[pltpu.VMEM((DIL_HEADS, tm, DIL_HD), jnp.float32) for _ in dils[:-1]],
        compiler_params=_params("parallel"),
        name="dil_proj",
    )(x2d, g, w_dil, cos_d, sin_d)


def _mla_attn_kernel(q_ref, k_ref, v_ref, o_ref, vext_scr, *, chunk):
    exp2_scale = math.log2(math.e) / math.sqrt(NOPE_DIM + ROPE_DIM)

    @pl.when(pl.program_id(2) == 0)
    def _():
        vext_scr[:, :V_DIM] = v_ref[0]
        vext_scr[:, V_DIM:] = jnp.ones((vext_scr.shape[0], V_DIM), jnp.bfloat16)

    q = q_ref[0]
    m = acc = None
    for c in range(k_ref.shape[1] // chunk):
        rows = slice(c * chunk, (c + 1) * chunk)
        s = lax.dot_general(q, k_ref[0, rows, :], _NT, preferred_element_type=jnp.float32)
        m_c = jnp.max(s, axis=-1, keepdims=True)
        m_new = m_c if m is None else jnp.maximum(m, m_c)
        p = jnp.exp2((s - m_new) * exp2_scale).astype(jnp.bfloat16)
        pv = _bf16_dot(p, vext_scr[rows, :])
        acc = pv if m is None else acc * jnp.exp2((m - m_new) * exp2_scale) + pv
        m = m_new
    o_ref[0] = acc[:, :V_DIM] / acc[:, V_DIM:V_DIM + 1]


def _mla_attn(q, k, v, tq=1024, chunk=512):
    b, s, _ = q.shape
    return pl.pallas_call(
        functools.partial(_mla_attn_kernel, chunk=chunk),
        grid=(b, MLA_HEADS, s // tq),
        in_specs=[
            pl.BlockSpec((1, tq, MLA_QK), lambda bi, h, i: (bi, i, h)),
            pl.BlockSpec((1, s, MLA_QK), lambda bi, h, i: (bi, 0, h)),
            pl.BlockSpec((1, s, V_DIM), lambda bi, h, i: (bi, 0, h)),
        ],
        out_specs=pl.BlockSpec((1, tq, V_DIM), lambda bi, h, i: (bi, i, h)),
        out_shape=jax.ShapeDtypeStruct((b, s, MLA_WIDTH), jnp.float32),
        scratch_shapes=[pltpu.VMEM((s, 2 * V_DIM), jnp.bfloat16)],
        compiler_params=_params("parallel", "parallel", "arbitrary"),
        name="mla_attn",
    )(q, k, v)


def _dil_attn_kernel(q_ref, kp_ref, kc_ref, kn_ref, vp_ref, vc_ref, vn_ref, o_ref, lse_ref, o_scr,
                     *, tq, sub, half, class_len, dilation):
    i = pl.program_id(1)
    r = pl.program_id(2)
    nk = sub + 2 * half
    k = jnp.concatenate([kp_ref[0, 0, 0], kc_ref[0, 0, 0], kn_ref[0, 0, 0]], axis=0)
    v = jnp.concatenate([vp_ref[0, 0, 0], vc_ref[0, 0, 0], vn_ref[0, 0, 0]], axis=0)
    inv_sqrt = 1.0 / math.sqrt(DIL_HD)
    lane = lax.broadcasted_iota(jnp.int32, (sub, LANES), 1)
    rel = lax.broadcasted_iota(jnp.int32, (sub, nk), 1) - half - lax.broadcasted_iota(jnp.int32, (sub, nk), 0)
    in_band = jnp.where(rel <= half, jnp.where(rel >= -half, 1, 0), 0)
    lse_tiles = []
    for st in range(tq // sub):
        q = q_ref[0, 0, 0, st * sub:(st + 1) * sub, :]
        k_pos = i * tq + st * sub - half + lax.broadcasted_iota(jnp.int32, (sub, nk), 1)
        valid = jnp.where(k_pos >= 0, jnp.where(k_pos < class_len, in_band, 0), 0)
        bias = jnp.where(valid > 0, 0.0, MASK_VALUE).astype(jnp.float32)
        lse_tile = jnp.zeros((sub, LANES), jnp.float32)
        for hd in range(DIL_HEADS):
            lanes = slice(hd * DIL_HD, (hd + 1) * DIL_HD)
            kh = k[st * sub:st * sub + nk, lanes]
            vh = v[st * sub:st * sub + nk, lanes]
            s = lax.dot_general(q[:, lanes], kh, _NT, preferred_element_type=jnp.float32) + bias
            m = jnp.max(s, axis=-1, keepdims=True)
            e = jnp.exp2((s - m) * (inv_sqrt * math.log2(math.e)))
            den = jnp.sum(e, axis=-1, keepdims=True)
            o_scr[hd, st * sub:(st + 1) * sub, :] = _bf16_dot(e.astype(jnp.bfloat16), vh) / den
            lse_tile = jnp.where(lane == hd, m * inv_sqrt + jnp.log(den), lse_tile)
        lse_tiles.append(lse_tile)
    lse = jnp.concatenate(lse_tiles, axis=0)

    for rs in range(dilation):
        @pl.when(r == rs)
        def _():
            rows = pl.ds(rs, tq, stride=dilation) if dilation > 1 else slice(None)
            for hd in range(DIL_HEADS):
                o_ref[0, hd, rows, :] = o_scr[hd]
            lse_ref[0, rows, :] = lse


def _dil_attn(qkv, window, tq=256, sub=128):
    _, batch, dilation, class_len, _ = qkv.shape
    half = window // (2 * dilation)
    tq = min(tq, class_len)
    sub = min(sub, tq)
    nb = tq // half
    last = class_len // half - 1

    def spec(which, rows, row_block):
        return pl.BlockSpec((1, 1, 1, rows, DIL_WIDTH), lambda b, i, r: (which, b, r, row_block(i), 0))

    cur = lambda i: i
    prev = lambda i: jnp.maximum(i * nb - 1, 0)
    nxt = lambda i: jnp.minimum((i + 1) * nb, last)
    return pl.pallas_call(
        functools.partial(_dil_attn_kernel, tq=tq, sub=sub, half=half, class_len=class_len,
                          dilation=dilation),
        grid=(batch, class_len // tq, dilation),
        in_specs=[
            spec(0, tq, cur),
            spec(1, half, prev), spec(1, tq, cur), spec(1, half, nxt),
            spec(2, half, prev), spec(2, tq, cur), spec(2, half, nxt),
        ],
        out_specs=[
            pl.BlockSpec((1, DIL_HEADS, tq * dilation, DIL_HD), lambda b, i, r: (b, 0, i, 0)),
            pl.BlockSpec((1, tq * dilation, LANES), lambda b, i, r: (b, i, 0)),
        ],
        out_shape=[
            jax.ShapeDtypeStruct((batch, DIL_HEADS, class_len * dilation, DIL_HD), jnp.float32),
            jax.ShapeDtypeStruct((batch, class_len * dilation, LANES), jnp.float32),
        ],
        scratch_shapes=[pltpu.VMEM((DIL_HEADS, tq, DIL_HD), jnp.float32)],
        compiler_params=_params("parallel", "parallel", "arbitrary"),
        name=f"dil_attn_d{dilation}",
    )(qkv, qkv, qkv, qkv, qkv, qkv, qkv)


def _mix_out_kernel(x_ref, oa_ref, o1_ref, o2_ref, o3_ref, l1_ref, l2_ref, l3_ref,
                    ga_ref, gb_ref, wo_ref, y_ref):
    l1, l2, l3 = l1_ref[...], l2_ref[...], l3_ref[...]
    mx = jnp.maximum(jnp.maximum(l1, l2), l3)
    e1, e2, e3 = jnp.exp(l1 - mx), jnp.exp(l2 - mx), jnp.exp(l3 - mx)
    tot = e1 + e2 + e3
    w1, w2, w3 = e1 / tot, e2 / tot, e3 / tot
    parts = []
    for hd in range(DIL_HEADS):
        col = slice(hd, hd + 1)
        parts.append(w1[:, col] * o1_ref[0, hd] + w2[:, col] * o2_ref[0, hd] + w3[:, col] * o3_ref[0, hd])
    o_b = jnp.concatenate(parts, axis=-1)
    mix_a = _rms(oa_ref[...], ga_ref[...]).astype(jnp.bfloat16)
    mix_b = _rms(o_b, gb_ref[...]).astype(jnp.bfloat16)
    y = _bf16_dot(mix_a, wo_ref[:MLA_WIDTH, :]) + _bf16_dot(mix_b, wo_ref[MLA_WIDTH:, :])
    y_ref[...] = x_ref[...] + y


def _mix_out(x2d, seq, o_a, dil, ga, gb, wo, tm=512):
    t = x2d.shape[0]
    tiles = seq // tm
    row = lambda i: (i, 0)
    const = lambda i: (0, 0)
    heads = pl.BlockSpec((1, DIL_HEADS, tm, DIL_HD), lambda i: (i // tiles, 0, i % tiles, 0))
    (o1, l1), (o2, l2), (o3, l3) = dil
    return pl.pallas_call(
        _mix_out_kernel,
        grid=(t // tm,),
        in_specs=[
            pl.BlockSpec((tm, D_MODEL), row),
            pl.BlockSpec((tm, MLA_WIDTH), row),
            heads,
            heads,
            heads,
            pl.BlockSpec((tm, LANES), row),
            pl.BlockSpec((tm, LANES), row),
            pl.BlockSpec((tm, LANES), row),
            pl.BlockSpec((1, MLA_WIDTH), const),
            pl.BlockSpec((1, DIL_WIDTH), const),
            _resident(wo.shape),
        ],
        out_specs=pl.BlockSpec((tm, D_MODEL), row),
        out_shape=jax.ShapeDtypeStruct((t, D_MODEL), jnp.float32),
        compiler_params=_params("parallel"),
        name="mix_out",
    )(x2d, o_a, o1, o2, o3, l1, l2, l3, ga, gb, wo)


def _mlp_kernel(x_ref, g_ref, wu_ref, wd_ref, gf_ref, y_ref, h_scr, acc_scr):
    k = pl.program_id(1)

    @pl.when(k == 0)
    def _():
        x = x_ref[...]
        h_scr[...] = _rms(x, g_ref[...]).astype(jnp.bfloat16)
        acc_scr[...] = x

    u = _bf16_dot(h_scr[...], wu_ref[...])
    a = jnp.square(jnp.maximum(u, 0.0)).astype(jnp.bfloat16)
    acc_scr[...] += _bf16_dot(a, wd_ref[...])

    @pl.when(k == pl.num_programs(1) - 1)
    def _():
        y_ref[...] = _rms(acc_scr[...], gf_ref[...])


def _mlp(x2d, g, wu, wd, gf, tm=512, tf=1024):
    t = x2d.shape[0]
    return pl.pallas_call(
        _mlp_kernel,
        grid=(t // tm, D_FF // tf),
        in_specs=[
            pl.BlockSpec((tm, D_MODEL), lambda i, k: (i, 0)),
            pl.BlockSpec((1, D_MODEL), lambda i, k: (0, 0)),
            pl.BlockSpec((D_MODEL, tf), lambda i, k: (0, k)),
            pl.BlockSpec((tf, D_MODEL), lambda i, k: (k, 0)),
            pl.BlockSpec((1, D_MODEL), lambda i, k: (0, 0)),
        ],
        out_specs=pl.BlockSpec((tm, D_MODEL), lambda i, k: (i, 0)),
        out_shape=jax.ShapeDtypeStruct((t, D_MODEL), jnp.float32),
        scratch_shapes=[
            pltpu.VMEM((tm, D_MODEL), jnp.bfloat16),
            pltpu.VMEM((tm, D_MODEL), jnp.float32),
        ],
        compiler_params=_params("parallel", "arbitrary"),
        name="mlp",
    )(x2d, g, wu, wd, gf)


def _rope_tables(seq):
    pos = jnp.arange(seq, dtype=jnp.float32)

    def cos_sin(dim):
        inv = 1.0 / (ROPE_THETA ** (jnp.arange(0, dim, 2, dtype=jnp.float32) / dim))
        ang = pos[:, None] * inv[None, :]
        return jnp.cos(ang), jnp.sin(ang)

    c, s = cos_sin(DIL_HD)
    cos_d = jnp.concatenate([c, c], axis=-1)
    sin_d = jnp.concatenate([-s, s], axis=-1)
    c, s = cos_sin(ROPE_DIM)
    z = jnp.zeros_like(c)
    cos_m = jnp.concatenate([c, z, c, z], axis=-1)
    sin_m = jnp.concatenate([-s, z, s, z], axis=-1)
    return cos_d, sin_d, cos_m, sin_m


def _spread_rope_cols(w):
    z = jnp.zeros(w.shape[:-1] + (ROPE_HALF,), w.dtype)
    return jnp.concatenate([w[..., :ROPE_HALF], z, w[..., ROPE_HALF:], z], axis=-1)


def _prepare_weights(w_in, w_q_b, w_kv_b, w_o, w_up, w_down):
    bf = jnp.bfloat16
    n_lora = Q_LORA + KV_LORA
    wm = jnp.concatenate([w_in[:, :n_lora], _spread_rope_cols(w_in[:, n_lora:n_lora + ROPE_DIM])],
                         axis=-1).astype(bf)
    w_dil = w_in[:, n_lora + ROPE_DIM:].astype(bf)
    wq = w_q_b.reshape(Q_LORA, MLA_HEADS, NOPE_DIM + ROPE_DIM)
    wq = jnp.concatenate([wq[..., :NOPE_DIM], _spread_rope_cols(wq[..., NOPE_DIM:])], axis=-1)
    wq = wq.reshape(Q_LORA, MLA_HEADS * MLA_QK).astype(bf)
    wkv = w_kv_b.reshape(KV_LORA, MLA_HEADS, NOPE_DIM + V_DIM)
    wk = wkv[..., :NOPE_DIM].reshape(KV_LORA, MLA_HEADS * NOPE_DIM).astype(bf)
    wv = wkv[..., NOPE_DIM:].reshape(KV_LORA, MLA_WIDTH).astype(bf)
    return wm, w_dil, wq, wk, wv, w_o.astype(bf), w_up.astype(bf), w_down.astype(bf)


def _trunk(x, gains, weights):
    attn_g, qa_g, kva_g, mla_g, dil_g, mlp_g, final_g = gains
    wm, w_dil, wq, wk, wv, wo, wu, wd = weights
    batch, seq, _ = x.shape
    x2d = x.reshape(batch * seq, D_MODEL)
    cos_d, sin_d, cos_m, sin_m = _rope_tables(seq)

    q, k, v = _mla_pre(x2d, seq, attn_g, wm, qa_g, kva_g, wq, wk, wv, cos_m, sin_m)
    qkv_d = _dil_proj(x2d, batch, seq, attn_g, w_dil, cos_d, sin_d)
    o_a = _mla_attn(q.reshape(batch, seq, -1), k.reshape(batch, seq, -1), v.reshape(batch, seq, -1))
    dil = []
    for qkv, (window, _) in zip(qkv_d, DIL_PATTERNS):
        o, lse = _dil_attn(qkv, window)
        dil.append((o, lse.reshape(batch * seq, LANES)))
    x1 = _mix_out(x2d, seq, o_a.reshape(batch * seq, MLA_WIDTH), dil, mla_g, dil_g, wo)
    y = _mlp(x1, mlp_g, wu, wd, final_g)
    return y.reshape(batch, seq, D_MODEL)


def kernel(x_prompt, x_sample, attn_norm_g, w_in, q_a_norm_g, w_q_b, kv_a_norm_g, w_kv_b,
           mla_out_norm_g, dil_out_norm_g, w_o, mlp_norm_g, w_up, w_down, final_norm_g):
    assert w_in.shape[0] == 1, "single-layer block"
    weights = _prepare_weights(w_in[0], w_q_b[0], w_kv_b[0], w_o[0], w_up[0], w_down[0])
    gains = (attn_norm_g[0][None], q_a_norm_g[0][None], kv_a_norm_g[0][None],
             mla_out_norm_g[0][None], dil_out_norm_g[0][None], mlp_norm_g[0][None],
             final_norm_g[None])
    return (_trunk(x_prompt, gains, weights), _trunk(x_sample, gains, weights))
```

```python
import functools
import math

import jax
import jax.numpy as jnp
from jax import lax
from jax.experimental import pallas as pl
from jax.experimental.pallas import tpu as pltpu

D_MODEL = 2048
EPS = 1e-6
ROPE_THETA = 10000.0
MLA_HEADS = 8
Q_LORA = 512
KV_LORA = 512
NOPE_DIM = 128
ROPE_DIM = 64
V_DIM = 128
MLA_WIDTH = MLA_HEADS * V_DIM
DIL_HEADS = 8
DIL_HD = 128
DIL_WIDTH = DIL_HEADS * DIL_HD
DIL_PATTERNS = ((128, 1), (512, 4), (2048, 16))
D_FF = 4 * D_MODEL

LANES = 128
VMEM_LIMIT_BYTES = 56 * 1024 * 1024
MXU_DIM = 256
ROW_CHUNK = 128

MLA_QK = NOPE_DIM + LANES
ROPE_HALF = ROPE_DIM // 2
MASK_VALUE = -1e30

_NT = (((1,), (1,)), ((), ()))


def _rms(x, g):
    return x * lax.rsqrt(jnp.mean(x * x, axis=-1, keepdims=True) + EPS) * g


def _rope128(x, cos, sin):
    return x * cos + pltpu.roll(x, LANES // 2, 1) * sin


def _bf16_dot(a, b):
    return jnp.dot(a, b, preferred_element_type=jnp.float32)


def _params(*sem):
    return pltpu.CompilerParams(dimension_semantics=sem, vmem_limit_bytes=VMEM_LIMIT_BYTES)


def _mla_pre_kernel(x_ref, g_ref, wm_ref, qg_ref, kvg_ref, wq_ref, wk_ref, wv_ref,
                    cos_ref, sin_ref, q_out, k_out, v_out):
    h = _rms(x_ref[...], g_ref[...]).astype(jnp.bfloat16)
    a = _bf16_dot(h, wm_ref[...])
    qn = _rms(a[:, :Q_LORA], qg_ref[...]).astype(jnp.bfloat16)
    kvn = _rms(a[:, Q_LORA:Q_LORA + KV_LORA], kvg_ref[...]).astype(jnp.bfloat16)
    cos, sin = cos_ref[...], sin_ref[...]
    k_rope = _rope128(a[:, Q_LORA + KV_LORA:], cos, sin).astype(jnp.bfloat16)
    q = _bf16_dot(qn, wq_ref[...])
    k_nope = _bf16_dot(kvn, wk_ref[...])
    for hd in range(MLA_HEADS):
        lo = hd * MLA_QK
        q_out[:, lo:lo + NOPE_DIM] = q[:, lo:lo + NOPE_DIM].astype(jnp.bfloat16)
        q_out[:, lo + NOPE_DIM:lo + MLA_QK] = _rope128(
            q[:, lo + NOPE_DIM:lo + MLA_QK], cos, sin).astype(jnp.bfloat16)
        k_out[:, lo:lo + NOPE_DIM] = k_nope[:, hd * NOPE_DIM:(hd + 1) * NOPE_DIM].astype(jnp.bfloat16)
        k_out[:, lo + NOPE_DIM:lo + MLA_QK] = k_rope
    v_out[...] = _bf16_dot(kvn, wv_ref[...]).astype(jnp.bfloat16)


def _resident(shape):
    return pl.BlockSpec(shape, lambda *_: (0,) * len(shape), pipeline_mode=pl.Buffered(1))


def _mla_pre(x2d, seq, g, wm, qg, kvg, wq, wk, wv, cos_m, sin_m, tm=512):
    t = x2d.shape[0]
    pos_blocks = seq // tm
    const = lambda i: (0, 0)
    row = lambda i: (i, 0)
    pos = lambda i: (i % pos_blocks, 0)
    return pl.pallas_call(
        _mla_pre_kernel,
        grid=(t // tm,),
        in_specs=[
            pl.BlockSpec((tm, D_MODEL), row),
            pl.BlockSpec((1, D_MODEL), const),
            _resident(wm.shape),
            pl.BlockSpec((1, Q_LORA), const),
            pl.BlockSpec((1, KV_LORA), const),
            _resident(wq.shape),
            _resident(wk.shape),
            _resident(wv.shape),
            pl.BlockSpec((tm, LANES), pos),
            pl.BlockSpec((tm, LANES), pos),
        ],
        out_specs=[
            pl.BlockSpec((tm, MLA_HEADS * MLA_QK), row),
            pl.BlockSpec((tm, MLA_HEADS * MLA_QK), row),
            pl.BlockSpec((tm, MLA_WIDTH), row),
        ],
        out_shape=[
            jax.ShapeDtypeStruct((t, MLA_HEADS * MLA_QK), jnp.bfloat16),
            jax.ShapeDtypeStruct((t, MLA_HEADS * MLA_QK), jnp.bfloat16),
            jax.ShapeDtypeStruct((t, MLA_WIDTH), jnp.bfloat16),
        ],
        compiler_params=_params("parallel"),
        name="mla_pre",
    )(x2d, g, wm, qg, kvg, wq, wk, wv, cos_m, sin_m)


def _dil_proj_kernel(x_ref, g_ref, w_ref, cos_ref, sin_ref, *refs):
    n = len(DIL_PATTERNS)
    out_refs, planes = refs[:n], refs[n:]
    dils = [d for _, d in DIL_PATTERNS]
    tm = x_ref.shape[0]
    h = _rms(x_ref[...], g_ref[...]).astype(jnp.bfloat16)
    cos, sin = cos_ref[...], sin_ref[...]
    for which in range(3):
        y = _bf16_dot(h, w_ref[:, which * DIL_WIDTH:(which + 1) * DIL_WIDTH])
        for hd in range(DIL_HEADS):
            lanes = slice(hd * DIL_HD, (hd + 1) * DIL_HD)
            yh = y[:, lanes]
            if which < 2:
                yh = _rope128(yh, cos, sin)
            out_refs[0][which, 0, 0, :, lanes] = yh.astype(jnp.bfloat16)
            planes[0][hd] = yh
            for lvl in range(1, n):
                d_prev, d = dils[lvl - 1], dils[lvl]
                ratio, rows_prev, rows = d // d_prev, tm // d_prev, tm // d
                for r_prev in range(d_prev):
                    for sub in range(ratio):
                        c = planes[lvl - 1][hd, pl.ds(r_prev * rows_prev + sub, rows, stride=ratio), :]
                        cls = d_prev * sub + r_prev
                        out_refs[lvl][which, 0, cls, :, lanes] = c.astype(jnp.bfloat16)
                        if lvl + 1 < n:
                            planes[lvl][hd, cls * rows:(cls + 1) * rows, :] = c


def _dil_proj(x2d, batch, seq, g, w_dil, cos_d, sin_d, tm=512):
    t = x2d.shape[0]
    tiles = seq // tm
    dils = [d for _, d in DIL_PATTERNS]
    assert dils[0] == 1 and all(b % a == 0 for a, b in zip(dils, dils[1:]))
    const = lambda i: (0, 0)
    pos = lambda i: (i % tiles, 0)
    return pl.pallas_call(
        _dil_proj_kernel,
        grid=(t // tm,),
        in_specs=[
            pl.BlockSpec((tm, D_MODEL), lambda i: (i, 0)),
            pl.BlockSpec((1, D_MODEL), const),
            _resident(w_dil.shape),
            pl.BlockSpec((tm, LANES), pos),
            pl.BlockSpec((tm, LANES), pos),
        ],
        out_specs=[
            pl.BlockSpec((3, 1, d, tm // d, DIL_WIDTH), lambda i: (0, i // tiles, 0, i % tiles, 0))
            for d in dils
        ],
        out_shape=[jax.ShapeDtypeStruct((3, batch, d, seq // d, DIL_WIDTH), jnp.bfloat16) for d in dils],
        scratch_shapes=[pltpu.VMEM((DIL_HEADS, tm, DIL_HD), jnp.float32) for _ in dils[:-1]],
        compiler_params=_params("parallel"),
        name="dil_proj",
    )(x2d, g, w_dil, cos_d, sin_d)


def _mla_attn_kernel(q_ref, k_ref, v_ref, o_ref, vext_scr, *, chunk):
    exp2_scale = math.log2(math.e) / math.sqrt(NOPE_DIM + ROPE_DIM)

    heads = vext_scr.shape[0]

    @pl.when(pl.program_id(2) == 0)
    def _():
        for hd in range(heads):
            vext_scr[hd, :, :V_DIM] = v_ref[0, :, hd * V_DIM:(hd + 1) * V_DIM]
            vext_scr[hd, :, V_DIM:] = jnp.ones((vext_scr.shape[1], V_DIM), jnp.bfloat16)

    for hd in range(heads):
        q = q_ref[0, :, hd * MLA_QK:(hd + 1) * MLA_QK]
        m = acc = None
        for c in range(k_ref.shape[1] // chunk):
            rows = slice(c * chunk, (c + 1) * chunk)
            s = lax.dot_general(q, k_ref[0, rows, hd * MLA_QK:(hd + 1) * MLA_QK], _NT,
                                preferred_element_type=jnp.float32)
            m_c = jnp.max(s, axis=-1, keepdims=True)
            m_new = m_c if m is None else jnp.maximum(m, m_c)
            p = jnp.exp2((s - m_new) * exp2_scale).astype(jnp.bfloat16)
            pv = _bf16_dot(p, vext_scr[hd, rows, :])
            acc = pv if m is None else acc * jnp.exp2((m - m_new) * exp2_scale) + pv
            m = m_new
        o_ref[0, :, hd * V_DIM:(hd + 1) * V_DIM] = acc[:, :V_DIM] / acc[:, V_DIM:V_DIM + 1]


def _mla_attn(q, k, v, tq=1024, chunk=512, heads=2):
    b, s, _ = q.shape
    return pl.pallas_call(
        functools.partial(_mla_attn_kernel, chunk=chunk),
        grid=(b, MLA_HEADS // heads, s // tq),
        in_specs=[
            pl.BlockSpec((1, tq, heads * MLA_QK), lambda bi, h, i: (bi, i, h)),
            pl.BlockSpec((1, s, heads * MLA_QK), lambda bi, h, i: (bi, 0, h)),
            pl.BlockSpec((1, s, heads * V_DIM), lambda bi, h, i: (bi, 0, h)),
        ],
        out_specs=pl.BlockSpec((1, tq, heads * V_DIM), lambda bi, h, i: (bi, i, h)),
        out_shape=jax.ShapeDtypeStruct((b, s, MLA_WIDTH), jnp.float32),
        scratch_shapes=[pltpu.VMEM((heads, s, 2 * V_DIM), jnp.bfloat16)],
        compiler_params=_params("parallel", "parallel", "arbitrary"),
        name="mla_attn",
    )(q, k, v)


def _dil_attn_kernel(q_ref, kp_ref, kc_ref, kn_ref, vp_ref, vc_ref, vn_ref, o_ref, lse_ref, o_scr,
                     *, tq, sub, half, class_len, dilation):
    i = pl.program_id(1)
    r = pl.program_id(2)
    nk = sub + 2 * half
    k = jnp.concatenate([kp_ref[0, 0, 0], kc_ref[0, 0, 0], kn_ref[0, 0, 0]], axis=0)
    v = jnp.concatenate([vp_ref[0, 0, 0], vc_ref[0, 0, 0], vn_ref[0, 0, 0]], axis=0)
    inv_sqrt = 1.0 / math.sqrt(DIL_HD)
    lane = lax.broadcasted_iota(jnp.int32, (sub, LANES), 1)
    rel = lax.broadcasted_iota(jnp.int32, (sub, nk), 1) - half - lax.broadcasted_iota(jnp.int32, (sub, nk), 0)
    in_band = jnp.where(rel <= half, jnp.where(rel >= -half, 1, 0), 0)
    lse_tiles = []
    for st in range(tq // sub):
        q = q_ref[0, 0, 0, st * sub:(st + 1) * sub, :]
        k_pos = i * tq + st * sub - half + lax.broadcasted_iota(jnp.int32, (sub, nk), 1)
        valid = jnp.where(k_pos >= 0, jnp.where(k_pos < class_len, in_band, 0), 0)
        bias = jnp.where(valid > 0, 0.0, MASK_VALUE).astype(jnp.float32)
        lse_tile = jnp.zeros((sub, LANES), jnp.float32)
        for hd in range(DIL_HEADS):
            lanes = slice(hd * DIL_HD, (hd + 1) * DIL_HD)
            kh = k[st * sub:st * sub + nk, lanes]
            vh = v[st * sub:st * sub + nk, lanes]
            s = lax.dot_general(q[:, lanes], kh, _NT, preferred_element_type=jnp.float32) + bias
            m = jnp.max(s, axis=-1, keepdims=True)
            e = jnp.exp2((s - m) * (inv_sqrt * math.log2(math.e)))
            den = jnp.sum(e, axis=-1, keepdims=True)
            o_scr[hd, st * sub:(st + 1) * sub, :] = _bf16_dot(e.astype(jnp.bfloat16), vh) / den
            lse_tile = jnp.where(lane == hd, m * inv_sqrt + jnp.log(den), lse_tile)
        lse_tiles.append(lse_tile)
    lse = jnp.concatenate(lse_tiles, axis=0)

    for rs in range(dilation):
        @pl.when(r == rs)
        def _():
            rows = pl.ds(rs, tq, stride=dilation) if dilation > 1 else slice(None)
            for hd in range(DIL_HEADS):
                o_ref[0, hd, rows, :] = o_scr[hd]
            lse_ref[0, rows, :] = lse


def _dil_attn(qkv, window, tq=256, sub=128):
    _, batch, dilation, class_len, _ = qkv.shape
    half = window // (2 * dilation)
    tq = min(tq, class_len)
    sub = min(sub, tq)
    nb = tq // half
    last = class_len // half - 1

    def spec(which, rows, row_block):
        return pl.BlockSpec((1, 1, 1, rows, DIL_WIDTH), lambda b, i, r: (which, b, r, row_block(i), 0))

    cur = lambda i: i
    prev = lambda i: jnp.maximum(i * nb - 1, 0)
    nxt = lambda i: jnp.minimum((i + 1) * nb, last)
    return pl.pallas_call(
        functools.partial(_dil_attn_kernel, tq=tq, sub=sub, half=half, class_len=class_len,
                          dilation=dilation),
        grid=(batch, class_len // tq, dilation),
        in_specs=[
            spec(0, tq, cur),
            spec(1, half, prev), spec(1, tq, cur), spec(1, half, nxt),
            spec(2, half, prev), spec(2, tq, cur), spec(2, half, nxt),
        ],
        out_specs=[
            pl.BlockSpec((1, DIL_HEADS, tq * dilation, DIL_HD), lambda b, i, r: (b, 0, i, 0)),
            pl.BlockSpec((1, tq * dilation, LANES), lambda b, i, r: (b, i, 0)),
        ],
        out_shape=[
            jax.ShapeDtypeStruct((batch, DIL_HEADS, class_len * dilation, DIL_HD), jnp.float32),
            jax.ShapeDtypeStruct((batch, class_len * dilation, LANES), jnp.float32),
        ],
        scratch_shapes=[pltpu.VMEM((DIL_HEADS, tq, DIL_HD), jnp.float32)],
        compiler_params=_params("parallel", "parallel", "arbitrary"),
        name=f"dil_attn_d{dilation}",
    )(qkv, qkv, qkv, qkv, qkv, qkv, qkv)


def _mix_out_kernel(x_ref, oa_ref, o1_ref, o2_ref, o3_ref, l1_ref, l2_ref, l3_ref,
                    ga_ref, gb_ref, wo_ref, y_ref):
    for c in range(x_ref.shape[0] // ROW_CHUNK):
        rows = slice(c * ROW_CHUNK, (c + 1) * ROW_CHUNK)
        l1, l2, l3 = l1_ref[rows, :], l2_ref[rows, :], l3_ref[rows, :]
        mx = jnp.maximum(jnp.maximum(l1, l2), l3)
        e1, e2, e3 = jnp.exp(l1 - mx), jnp.exp(l2 - mx), jnp.exp(l3 - mx)
        tot = e1 + e2 + e3
        w1, w2 = e1 / tot, e2 / tot
        parts = []
        for hd in range(DIL_HEADS):
            col = slice(hd, hd + 1)
            o3 = o3_ref[0, hd, rows, :]
            parts.append(o3 + w1[:, col] * (o1_ref[0, hd, rows, :] - o3)
                         + w2[:, col] * (o2_ref[0, hd, rows, :] - o3))
        o_b = jnp.concatenate(parts, axis=-1)
        mix_a = _rms(oa_ref[rows, :], ga_ref[...]).astype(jnp.bfloat16)
        mix_b = _rms(o_b, gb_ref[...]).astype(jnp.bfloat16)
        y = _bf16_dot(mix_a, wo_ref[:MLA_WIDTH, :]) + _bf16_dot(mix_b, wo_ref[MLA_WIDTH:, :])
        y_ref[rows, :] = x_ref[rows, :] + y


def _mix_out(x2d, seq, o_a, dil, ga, gb, wo, tm=512):
    t = x2d.shape[0]
    tiles = seq // tm
    row = lambda i: (i, 0)
    const = lambda i: (0, 0)
    heads = pl.BlockSpec((1, DIL_HEADS, tm, DIL_HD), lambda i: (i // tiles, 0, i % tiles, 0))
    (o1, l1), (o2, l2), (o3, l3) = dil
    return pl.pallas_call(
        _mix_out_kernel,
        grid=(t // tm,),
        in_specs=[
            pl.BlockSpec((tm, D_MODEL), row),
            pl.BlockSpec((tm, MLA_WIDTH), row),
            heads,
            heads,
            heads,
            pl.BlockSpec((tm, LANES), row),
            pl.BlockSpec((tm, LANES), row),
            pl.BlockSpec((tm, LANES), row),
            pl.BlockSpec((1, MLA_WIDTH), const),
            pl.BlockSpec((1, DIL_WIDTH), const),
            _resident(wo.shape),
        ],
        out_specs=pl.BlockSpec((tm, D_MODEL), row),
        out_shape=jax.ShapeDtypeStruct((t, D_MODEL), jnp.float32),
        compiler_params=_params("parallel"),
        name="mix_out",
    )(x2d, o_a, o1, o2, o3, l1, l2, l3, ga, gb, wo)


def _mlp_kernel(x_ref, g_ref, wu_ref, wd_ref, gf_ref, y_ref, h_scr, acc_scr):
    k = pl.program_id(1)

    @pl.when(k == 0)
    def _():
        x = x_ref[...]
        h_scr[...] = _rms(x, g_ref[...]).astype(jnp.bfloat16)
        acc_scr[...] = x

    u = _bf16_dot(h_scr[...], wu_ref[...])
    a = jnp.square(jnp.maximum(u, 0.0)).astype(jnp.bfloat16)
    acc_scr[...] += _bf16_dot(a, wd_ref[...])

    @pl.when(k == pl.num_programs(1) - 1)
    def _():
        y_ref[...] = _rms(acc_scr[...], gf_ref[...])


def _mlp(x2d, g, wu, wd, gf, tm=512, tf=1024):
    t = x2d.shape[0]
    return pl.pallas_call(
        _mlp_kernel,
        grid=(t // tm, D_FF // tf),
        in_specs=[
            pl.BlockSpec((tm, D_MODEL), lambda i, k: (i, 0)),
            pl.BlockSpec((1, D_MODEL), lambda i, k: (0, 0)),
            pl.BlockSpec((D_MODEL, tf), lambda i, k: (0, k)),
            pl.BlockSpec((tf, D_MODEL), lambda i, k: (k, 0)),
            pl.BlockSpec((1, D_MODEL), lambda i, k: (0, 0)),
        ],
        out_specs=pl.BlockSpec((tm, D_MODEL), lambda i, k: (i, 0)),
        out_shape=jax.ShapeDtypeStruct((t, D_MODEL), jnp.float32),
        scratch_shapes=[
            pltpu.VMEM((tm, D_MODEL), jnp.bfloat16),
            pltpu.VMEM((tm, D_MODEL), jnp.float32),
        ],
        compiler_params=_params("parallel", "arbitrary"),
        name="mlp",
    )(x2d, g, wu, wd, gf)


def _rope_tables(seq):
    pos = jnp.arange(seq, dtype=jnp.float32)

    def cos_sin(dim):
        inv = 1.0 / (ROPE_THETA ** (jnp.arange(0, dim, 2, dtype=jnp.float32) / dim))
        ang = pos[:, None] * inv[None, :]
        return jnp.cos(ang), jnp.sin(ang)

    c, s = cos_sin(DIL_HD)
    cos_d = jnp.concatenate([c, c], axis=-1)
    sin_d = jnp.concatenate([-s, s], axis=-1)
    c, s = cos_sin(ROPE_DIM)
    z = jnp.zeros_like(c)
    cos_m = jnp.concatenate([c, z, c, z], axis=-1)
    sin_m = jnp.concatenate([-s, z, s, z], axis=-1)
    return cos_d, sin_d, cos_m, sin_m


def _spread_rope_cols(w):
    z = jnp.zeros(w.shape[:-1] + (ROPE_HALF,), w.dtype)
    return jnp.concatenate([w[..., :ROPE_HALF], z, w[..., ROPE_HALF:], z], axis=-1)


def _prepare_weights(w_in, w_q_b, w_kv_b, w_o, w_up, w_down):
    bf = jnp.bfloat16
    n_lora = Q_LORA + KV_LORA
    wm = jnp.concatenate([w_in[:, :n_lora], _spread_rope_cols(w_in[:, n_lora:n_lora + ROPE_DIM])],
                         axis=-1).astype(bf)
    w_dil = w_in[:, n_lora + ROPE_DIM:].astype(bf)
    wq = w_q_b.reshape(Q_LORA, MLA_HEADS, NOPE_DIM + ROPE_DIM)
    wq = jnp.concatenate([wq[..., :NOPE_DIM], _spread_rope_cols(wq[..., NOPE_DIM:])], axis=-1)
    wq = wq.reshape(Q_LORA, MLA_HEADS * MLA_QK).astype(bf)
    wkv = w_kv_b.reshape(KV_LORA, MLA_HEADS, NOPE_DIM + V_DIM)
    wk = wkv[..., :NOPE_DIM].reshape(KV_LORA, MLA_HEADS * NOPE_DIM).astype(bf)
    wv = wkv[..., NOPE_DIM:].reshape(KV_LORA, MLA_WIDTH).astype(bf)
    return wm, w_dil, wq, wk, wv, w_o.astype(bf), w_up.astype(bf), w_down.astype(bf)


def _trunk(x, gains, weights):
    attn_g, qa_g, kva_g, mla_g, dil_g, mlp_g, final_g = gains
    wm, w_dil, wq, wk, wv, wo, wu, wd = weights
    batch, seq, _ = x.shape
    x2d = x.reshape(batch * seq, D_MODEL)
    cos_d, sin_d, cos_m, sin_m = _rope_tables(seq)

    q, k, v = _mla_pre(x2d, seq, attn_g, wm, qa_g, kva_g, wq, wk, wv, cos_m, sin_m)
    qkv_d = _dil_proj(x2d, batch, seq, attn_g, w_dil, cos_d, sin_d)
    o_a = _mla_attn(q.reshape(batch, seq, -1), k.reshape(batch, seq, -1), v.reshape(batch, seq, -1))
    dil = []
    for qkv, (window, _) in zip(qkv_d, DIL_PATTERNS):
        o, lse = _dil_attn(qkv, window)
        dil.append((o, lse.reshape(batch * seq, LANES)))
    x1 = _mix_out(x2d, seq, o_a.reshape(batch * seq, MLA_WIDTH), dil, mla_g, dil_g, wo)
    y = _mlp(x1, mlp_g, wu, wd, final_g)
    return y.reshape(batch, seq, D_MODEL)


def kernel(x_prompt, x_sample, attn_norm_g, w_in, q_a_norm_g, w_q_b, kv_a_norm_g, w_kv_b,
           mla_out_norm_g, dil_out_norm_g, w_o, mlp_norm_g, w_up, w_down, final_norm_g):
    assert w_in.shape[0] == 1, "single-layer block"
    weights = _prepare_weights(w_in[0], w_q_b[0], w_kv_b[0], w_o[0], w_up[0], w_down[0])
    gains = (attn_norm_g[0][None], q_a_norm_g[0][None], kv_a_norm_g[0][None],
             mla_out_norm_g[0][None], dil_out_norm_g[0][None], mlp_norm_g[0][None],
             final_norm_g[None])
    return (_trunk(x_prompt, gains, weights), _trunk(x_sample, gains, weights))
```

```python
import functools
import math

import jax
import jax.numpy as jnp
from jax import lax
from jax.experimental import pallas as pl
from jax.experimental.pallas import tpu as pltpu

D_MODEL = 2048
EPS = 1e-6
ROPE_THETA = 10000.0
MLA_HEADS = 8
Q_LORA = 512
KV_LORA = 512
NOPE_DIM = 128
ROPE_DIM = 64
V_DIM = 128
MLA_WIDTH = MLA_HEADS * V_DIM
DIL_HEADS = 8
DIL_HD = 128
DIL_WIDTH = DIL_HEADS * DIL_HD
DIL_PATTERNS = ((128, 1), (512, 4), (2048, 16))
D_FF = 4 * D_MODEL

LANES = 128
VMEM_LIMIT_BYTES = 56 * 1024 * 1024
MXU_DIM = 256
ROW_CHUNK = 128

MLA_QK = NOPE_DIM + LANES
ROPE_HALF = ROPE_DIM // 2
MASK_VALUE = -1e30

_NT = (((1,), (1,)), ((), ()))


def _rms(x, g):
    return x * lax.rsqrt(jnp.mean(x * x, axis=-1, keepdims=True) + EPS) * g


def _rope128(x, cos, sin):
    return x * cos + pltpu.roll(x, LANES // 2, 1) * sin


def _bf16_dot(a, b):
    return jnp.dot(a, b, preferred_element_type=jnp.float32)


def _params(*sem):
    return pltpu.CompilerParams(dimension_semantics=sem, vmem_limit_bytes=VMEM_LIMIT_BYTES)


def _mla_pre_kernel(x_ref, g_ref, wm_ref, qg_ref, kvg_ref, wq_ref, wk_ref, wv_ref,
                    cos_ref, sin_ref, q_out, k_out, v_out):
    h = _rms(x_ref[...], g_ref[...]).astype(jnp.bfloat16)
    a = _bf16_dot(h, wm_ref[...])
    qn = _rms(a[:, :Q_LORA], qg_ref[...]).astype(jnp.bfloat16)
    kvn = _rms(a[:, Q_LORA:Q_LORA + KV_LORA], kvg_ref[...]).astype(jnp.bfloat16)
    cos, sin = cos_ref[...], sin_ref[...]
    k_rope = _rope128(a[:, Q_LORA + KV_LORA:], cos, sin).astype(jnp.bfloat16)
    q = _bf16_dot(qn, wq_ref[...])
    k_nope = _bf16_dot(kvn, wk_ref[...])
    for hd in range(MLA_HEADS):
        lo = hd * MLA_QK
        q_out[:, lo:lo + NOPE_DIM] = q[:, lo:lo + NOPE_DIM].astype(jnp.bfloat16)
        q_out[:, lo + NOPE_DIM:lo + MLA_QK] = _rope128(
            q[:, lo + NOPE_DIM:lo + MLA_QK], cos, sin).astype(jnp.bfloat16)
        k_out[:, lo:lo + NOPE_DIM] = k_nope[:, hd * NOPE_DIM:(hd + 1) * NOPE_DIM].astype(jnp.bfloat16)
        k_out[:, lo + NOPE_DIM:lo + MLA_QK] = k_rope
    v_out[...] = _bf16_dot(kvn, wv_ref[...]).astype(jnp.bfloat16)


def _resident(shape):
    return pl.BlockSpec(shape, lambda *_: (0,) * len(shape), pipeline_mode=pl.Buffered(1))


def _mla_pre(x2d, seq, g, wm, qg, kvg, wq, wk, wv, cos_m, sin_m, tm=512):
    t = x2d.shape[0]
    pos_blocks = seq // tm
    const = lambda i: (0, 0)
    row = lambda i: (i, 0)
    pos = lambda i: (i % pos_blocks, 0)
    return pl.pallas_call(
        _mla_pre_kernel,
        grid=(t // tm,),
        in_specs=[
            pl.BlockSpec((tm, D_MODEL), row),
            pl.BlockSpec((1, D_MODEL), const),
            _resident(wm.shape),
            pl.BlockSpec((1, Q_LORA), const),
            pl.BlockSpec((1, KV_LORA), const),
            _resident(wq.shape),
            _resident(wk.shape),
            _resident(wv.shape),
            pl.BlockSpec((tm, LANES), pos),
            pl.BlockSpec((tm, LANES), pos),
        ],
        out_specs=[
            pl.BlockSpec((tm, MLA_HEADS * MLA_QK), row),
            pl.BlockSpec((tm, MLA_HEADS * MLA_QK), row),
            pl.BlockSpec((tm, MLA_WIDTH), row),
        ],
        out_shape=[
            jax.ShapeDtypeStruct((t, MLA_HEADS * MLA_QK), jnp.bfloat16),
            jax.ShapeDtypeStruct((t, MLA_HEADS * MLA_QK), jnp.bfloat16),
            jax.ShapeDtypeStruct((t, MLA_WIDTH), jnp.bfloat16),
        ],
        compiler_params=_params("parallel"),
        name="mla_pre",
    )(x2d, g, wm, qg, kvg, wq, wk, wv, cos_m, sin_m)


def _dil_proj_kernel(x_ref, g_ref, w_ref, cos_ref, sin_ref, *refs):
    n = len(DIL_PATTERNS)
    out_refs, planes = refs[:n], refs[n:]
    dils = [d for _, d in DIL_PATTERNS]
    tm = x_ref.shape[0]
    h = _rms(x_ref[...], g_ref[...]).astype(jnp.bfloat16)
    cos, sin = cos_ref[...], sin_ref[...]
    for which in range(3):
        y = _bf16_dot(h, w_ref[:, which * DIL_WIDTH:(which + 1) * DIL_WIDTH])
        for hd in range(DIL_HEADS):
            lanes = slice(hd * DIL_HD, (hd + 1) * DIL_HD)
            yh = y[:, lanes]
            if which < 2:
                yh = _rope128(yh, cos, sin)
            out_refs[0][which, 0, 0, :, lanes] = yh.astype(jnp.bfloat16)
            planes[0][hd] = yh
            for lvl in range(1, n):
                d_prev, d = dils[lvl - 1], dils[lvl]
                ratio, rows_prev, rows = d // d_prev, tm // d_prev, tm // d
                for r_prev in range(d_prev):
                    for sub in range(ratio):
                        c = planes[lvl - 1][hd, pl.ds(r_prev * rows_prev + sub, rows, stride=ratio), :]
                        cls = d_prev * sub + r_prev
                        out_refs[lvl][which, 0, cls, :, lanes] = c.astype(jnp.bfloat16)
                        if lvl + 1 < n:
                            planes[lvl][hd, cls * rows:(cls + 1) * rows, :] = c


def _dil_proj(x2d, batch, seq, g, w_dil, cos_d, sin_d, tm=512):
    t = x2d.shape[0]
    tiles = seq // tm
    dils = [d for _, d in DIL_PATTERNS]
    assert dils[0] == 1 and all(b % a == 0 for a, b in zip(dils, dils[1:]))
    const = lambda i: (0, 0)
    pos = lambda i: (i % tiles, 0)
    return pl.pallas_call(
        _dil_proj_kernel,
        grid=(t // tm,),
        in_specs=[
            pl.BlockSpec((tm, D_MODEL), lambda i: (i, 0)),
            pl.BlockSpec((1, D_MODEL), const),
            _resident(w_dil.shape),
            pl.BlockSpec((tm, LANES), pos),
            pl.BlockSpec((tm, LANES), pos),
        ],
        out_specs=[
            pl.BlockSpec((3, 1, d, tm // d, DIL_WIDTH), lambda i: (0, i // tiles, 0, i % tiles, 0))
            for d in dils
        ],
        out_shape=[jax.ShapeDtypeStruct((3, batch, d, seq // d, DIL_WIDTH), jnp.bfloat16) for d in dils],
        scratch_shapes=[pltpu.VMEM((DIL_HEADS, tm, DIL_HD), jnp.float32) for _ in dils[:-1]],
        compiler_params=_params("parallel"),
        name="dil_proj",
    )(x2d, g, w_dil, cos_d, sin_d)


def _mla_attn_kernel(q_ref, k_ref, v_ref, o_ref, vext_scr, *, chunk):
    exp2_scale = math.log2(math.e) / math.sqrt(NOPE_DIM + ROPE_DIM)

    heads = vext_scr.shape[0]

    @pl.when(pl.program_id(2) == 0)
    def _():
        for hd in range(heads):
            vext_scr[hd, :, :V_DIM] = v_ref[0, :, hd * V_DIM:(hd + 1) * V_DIM]
            vext_scr[hd, :, V_DIM:] = jnp.ones((vext_scr.shape[1], V_DIM), jnp.bfloat16)

    for hd in range(heads):
        q = q_ref[0, :, hd * MLA_QK:(hd + 1) * MLA_QK]
        m = acc = None
        for c in range(k_ref.shape[1] // chunk):
            rows = slice(c * chunk, (c + 1) * chunk)
            s = lax.dot_general(q, k_ref[0, rows, hd * MLA_QK:(hd + 1) * MLA_QK], _NT,
                                preferred_element_type=jnp.float32)
            m_c = jnp.max(s, axis=-1, keepdims=True)
            m_new = m_c if m is None else jnp.maximum(m, m_c)
            p = jnp.exp2((s - m_new) * exp2_scale).astype(jnp.bfloat16)
            pv = _bf16_dot(p, vext_scr[hd, rows, :])
            acc = pv if m is None else acc * jnp.exp2((m - m_new) * exp2_scale) + pv
            m = m_new
        o_ref[0, :, hd * V_DIM:(hd + 1) * V_DIM] = acc[:, :V_DIM] / acc[:, V_DIM:V_DIM + 1]


def _mla_attn(q, k, v, tq=1024, chunk=512, heads=2):
    b, s, _ = q.shape
    return pl.pallas_call(
        functools.partial(_mla_attn_kernel, chunk=chunk),
        grid=(b, MLA_HEADS // heads, s // tq),
        in_specs=[
            pl.BlockSpec((1, tq, heads * MLA_QK), lambda bi, h, i: (bi, i, h)),
            pl.BlockSpec((1, s, heads * MLA_QK), lambda bi, h, i: (bi, 0, h)),
            pl.BlockSpec((1, s, heads * V_DIM), lambda bi, h, i: (bi, 0, h)),
        ],
        out_specs=pl.BlockSpec((1, tq, heads * V_DIM), lambda bi, h, i: (bi, i, h)),
        out_shape=jax.ShapeDtypeStruct((b, s, MLA_WIDTH), jnp.float32),
        scratch_shapes=[pltpu.VMEM((heads, s, 2 * V_DIM), jnp.bfloat16)],
        compiler_params=_params("parallel", "parallel", "arbitrary"),
        name="mla_attn",
    )(q, k, v)


def _dil_attn_kernel(q_ref, kp_ref, kc_ref, kn_ref, vp_ref, vc_ref, vn_ref, o_ref, lse_ref, o_scr, lse_scr,
                     *, tq, sub, half, class_len, dilation):
    i = pl.program_id(1)
    group = pl.program_id(2)
    classes = q_ref.shape[2]
    nk = sub + 2 * half
    inv_sqrt = 1.0 / math.sqrt(DIL_HD)
    lane = lax.broadcasted_iota(jnp.int32, (sub, LANES), 1)
    rel = lax.broadcasted_iota(jnp.int32, (sub, nk), 1) - half - lax.broadcasted_iota(jnp.int32, (sub, nk), 0)
    in_band = jnp.where(rel <= half, jnp.where(rel >= -half, 1, 0), 0)
    ones = jnp.ones((nk, DIL_HD), jnp.bfloat16)
    for cl in range(classes):
        k = jnp.concatenate([kp_ref[0, 0, cl], kc_ref[0, 0, cl], kn_ref[0, 0, cl]], axis=0)
        v = jnp.concatenate([vp_ref[0, 0, cl], vc_ref[0, 0, cl], vn_ref[0, 0, cl]], axis=0)
        for st in range(tq // sub):
            rows = slice(st * sub, (st + 1) * sub)
            q = q_ref[0, 0, cl, rows, :]
            k_pos = i * tq + st * sub - half + lax.broadcasted_iota(jnp.int32, (sub, nk), 1)
            valid = jnp.where(k_pos >= 0, jnp.where(k_pos < class_len, in_band, 0), 0)
            bias = jnp.where(valid > 0, 0.0, MASK_VALUE).astype(jnp.float32)
            lse_tile = jnp.zeros((sub, LANES), jnp.float32)
            for hd in range(DIL_HEADS):
                lanes = slice(hd * DIL_HD, (hd + 1) * DIL_HD)
                kh = k[st * sub:st * sub + nk, lanes]
                vh = v[st * sub:st * sub + nk, lanes]
                s = lax.dot_general(q[:, lanes], kh, _NT, preferred_element_type=jnp.float32) + bias
                m = jnp.max(s, axis=-1, keepdims=True)
                e = jnp.exp2((s - m) * (inv_sqrt * math.log2(math.e))).astype(jnp.bfloat16)
                pv = _bf16_dot(e, jnp.concatenate([vh, ones], axis=1))
                den = pv[:, DIL_HD:]
                o_scr[hd, cl, rows, :] = pv[:, :DIL_HD] / den
                lse_tile = jnp.where(lane == hd, m * inv_sqrt + jnp.log(den), lse_tile)
            lse_scr[cl, rows, :] = lse_tile

    for gs in range(dilation // classes):
        @pl.when(group == gs)
        def _():
            for cl in range(classes):
                r = gs * classes + cl
                rows = pl.ds(r, tq, stride=dilation) if dilation > 1 else slice(None)
                for hd in range(DIL_HEADS):
                    o_ref[0, hd, rows, :] = o_scr[hd, cl]
                lse_ref[0, rows, :] = lse_scr[cl]


def _dil_attn(qkv, window, rows_per_step=512, sub=128):
    _, batch, dilation, class_len, _ = qkv.shape
    half = window // (2 * dilation)
    tq = min(rows_per_step, class_len)
    classes = min(dilation, rows_per_step // tq)
    sub = min(sub, tq)
    nb = tq // half
    last = class_len // half - 1

    def spec(which, rows, row_block):
        return pl.BlockSpec((1, 1, classes, rows, DIL_WIDTH), lambda b, i, g: (which, b, g, row_block(i), 0))

    cur = lambda i: i
    prev = lambda i: jnp.maximum(i * nb - 1, 0)
    nxt = lambda i: jnp.minimum((i + 1) * nb, last)
    return pl.pallas_call(
        functools.partial(_dil_attn_kernel, tq=tq, sub=sub, half=half, class_len=class_len,
                          dilation=dilation),
        grid=(batch, class_len // tq, dilation // classes),
        in_specs=[
            spec(0, tq, cur),
            spec(1, half, prev), spec(1, tq, cur), spec(1, half, nxt),
            spec(2, half, prev), spec(2, tq, cur), spec(2, half, nxt),
        ],
        out_specs=[
            pl.BlockSpec((1, DIL_HEADS, tq * dilation, DIL_HD), lambda b, i, g: (b, 0, i, 0)),
            pl.BlockSpec((1, tq * dilation, LANES), lambda b, i, g: (b, i, 0)),
        ],
        out_shape=[
            jax.ShapeDtypeStruct((batch, DIL_HEADS, class_len * dilation, DIL_HD), jnp.float32),
            jax.ShapeDtypeStruct((batch, class_len * dilation, LANES), jnp.float32),
        ],
        scratch_shapes=[
            pltpu.VMEM((DIL_HEADS, classes, tq, DIL_HD), jnp.float32),
            pltpu.VMEM((classes, tq, LANES), jnp.float32),
        ],
        compiler_params=_params("parallel", "parallel", "arbitrary"),
        name=f"dil_attn_d{dilation}",
    )(qkv, qkv, qkv, qkv, qkv, qkv, qkv)


def _mix_out_kernel(x_ref, oa_ref, o1_ref, o2_ref, o3_ref, l1_ref, l2_ref, l3_ref,
                    ga_ref, gb_ref, wo_ref, y_ref):
    for c in range(x_ref.shape[0] // ROW_CHUNK):
        rows = slice(c * ROW_CHUNK, (c + 1) * ROW_CHUNK)
        l1, l2, l3 = l1_ref[rows, :], l2_ref[rows, :], l3_ref[rows, :]
        mx = jnp.maximum(jnp.maximum(l1, l2), l3)
        e1, e2, e3 = jnp.exp(l1 - mx), jnp.exp(l2 - mx), jnp.exp(l3 - mx)
        tot = e1 + e2 + e3
        w1, w2 = e1 / tot, e2 / tot
        parts = []
        for hd in range(DIL_HEADS):
            col = slice(hd, hd + 1)
            o3 = o3_ref[0, hd, rows, :]
            parts.append(o3 + w1[:, col] * (o1_ref[0, hd, rows, :] - o3)
                         + w2[:, col] * (o2_ref[0, hd, rows, :] - o3))
        o_b = jnp.concatenate(parts, axis=-1)
        mix_a = _rms(oa_ref[rows, :], ga_ref[...]).astype(jnp.bfloat16)
        mix_b = _rms(o_b, gb_ref[...]).astype(jnp.bfloat16)
        y = _bf16_dot(mix_a, wo_ref[:MLA_WIDTH, :]) + _bf16_dot(mix_b, wo_ref[MLA_WIDTH:, :])
        y_ref[rows, :] = x_ref[rows, :] + y


def _mix_out(x2d, seq, o_a, dil, ga, gb, wo, tm=512):
    t = x2d.shape[0]
    tiles = seq // tm
    row = lambda i: (i, 0)
    const = lambda i: (0, 0)
    heads = pl.BlockSpec((1, DIL_HEADS, tm, DIL_HD), lambda i: (i // tiles, 0, i % tiles, 0))
    (o1, l1), (o2, l2), (o3, l3) = dil
    return pl.pallas_call(
        _mix_out_kernel,
        grid=(t // tm,),
        in_specs=[
            pl.BlockSpec((tm, D_MODEL), row),
            pl.BlockSpec((tm, MLA_WIDTH), row),
            heads,
            heads,
            heads,
            pl.BlockSpec((tm, LANES), row),
            pl.BlockSpec((tm, LANES), row),
            pl.BlockSpec((tm, LANES), row),
            pl.BlockSpec((1, MLA_WIDTH), const),
            pl.BlockSpec((1, DIL_WIDTH), const),
            _resident(wo.shape),
        ],
        out_specs=pl.BlockSpec((tm, D_MODEL), row),
        out_shape=jax.ShapeDtypeStruct((t, D_MODEL), jnp.float32),
        compiler_params=_params("parallel"),
        name="mix_out",
    )(x2d, o_a, o1, o2, o3, l1, l2, l3, ga, gb, wo)


def _mlp_kernel(x_ref, g_ref, wu_ref, wd_ref, gf_ref, y_ref, h_scr, acc_scr):
    k = pl.program_id(1)

    @pl.when(k == 0)
    def _():
        x = x_ref[...]
        h_scr[...] = _rms(x, g_ref[...]).astype(jnp.bfloat16)
        acc_scr[...] = x

    u = _bf16_dot(h_scr[...], wu_ref[...])
    a = jnp.square(jnp.maximum(u, 0.0)).astype(jnp.bfloat16)
    acc_scr[...] += _bf16_dot(a, wd_ref[...])

    @pl.when(k == pl.num_programs(1) - 1)
    def _():
        y_ref[...] = _rms(acc_scr[...], gf_ref[...])


def _mlp(x2d, g, wu, wd, gf, tm=512, tf=1024):
    t = x2d.shape[0]
    return pl.pallas_call(
        _mlp_kernel,
        grid=(t // tm, D_FF // tf),
        in_specs=[
            pl.BlockSpec((tm, D_MODEL), lambda i, k: (i, 0)),
            pl.BlockSpec((1, D_MODEL), lambda i, k: (0, 0)),
            pl.BlockSpec((D_MODEL, tf), lambda i, k: (0, k)),
            pl.BlockSpec((tf, D_MODEL), lambda i, k: (k, 0)),
            pl.BlockSpec((1, D_MODEL), lambda i, k: (0, 0)),
        ],
        out_specs=pl.BlockSpec((tm, D_MODEL), lambda i, k: (i, 0)),
        out_shape=jax.ShapeDtypeStruct((t, D_MODEL), jnp.float32),
        scratch_shapes=[
            pltpu.VMEM((tm, D_MODEL), jnp.bfloat16),
            pltpu.VMEM((tm, D_MODEL), jnp.float32),
        ],
        compiler_params=_params("parallel", "arbitrary"),
        name="mlp",
    )(x2d, g, wu, wd, gf)


def _rope_tables(seq):
    pos = jnp.arange(seq, dtype=jnp.float32)

    def cos_sin(dim):
        inv = 1.0 / (ROPE_THETA ** (jnp.arange(0, dim, 2, dtype=jnp.float32) / dim))
        ang = pos[:, None] * inv[None, :]
        return jnp.cos(ang), jnp.sin(ang)

    c, s = cos_sin(DIL_HD)
    cos_d = jnp.concatenate([c, c], axis=-1)
    sin_d = jnp.concatenate([-s, s], axis=-1)
    c, s = cos_sin(ROPE_DIM)
    z = jnp.zeros_like(c)
    cos_m = jnp.concatenate([c, z, c, z], axis=-1)
    sin_m = jnp.concatenate([-s, z, s, z], axis=-1)
    return cos_d, sin_d, cos_m, sin_m


def _spread_rope_cols(w):
    z = jnp.zeros(w.shape[:-1] + (ROPE_HALF,), w.dtype)
    return jnp.concatenate([w[..., :ROPE_HALF], z, w[..., ROPE_HALF:], z], axis=-1)


def _prepare_weights(w_in, w_q_b, w_kv_b, w_o, w_up, w_down):
    bf = jnp.bfloat16
    n_lora = Q_LORA + KV_LORA
    wm = jnp.concatenate([w_in[:, :n_lora], _spread_rope_cols(w_in[:, n_lora:n_lora + ROPE_DIM])],
                         axis=-1).astype(bf)
    w_dil = w_in[:, n_lora + ROPE_DIM:].astype(bf)
    wq = w_q_b.reshape(Q_LORA, MLA_HEADS, NOPE_DIM + ROPE_DIM)
    wq = jnp.concatenate([wq[..., :NOPE_DIM], _spread_rope_cols(wq[..., NOPE_DIM:])], axis=-1)
    wq = wq.reshape(Q_LORA, MLA_HEADS * MLA_QK).astype(bf)
    wkv = w_kv_b.reshape(KV_LORA, MLA_HEADS, NOPE_DIM + V_DIM)
    wk = wkv[..., :NOPE_DIM].reshape(KV_LORA, MLA_HEADS * NOPE_DIM).astype(bf)
    wv = wkv[..., NOPE_DIM:].reshape(KV_LORA, MLA_WIDTH).astype(bf)
    return wm, w_dil, wq, wk, wv, w_o.astype(bf), w_up.astype(bf), w_down.astype(bf)


def _trunk(x, gains, weights):
    attn_g, qa_g, kva_g, mla_g, dil_g, mlp_g, final_g = gains
    wm, w_dil, wq, wk, wv, wo, wu, wd = weights
    batch, seq, _ = x.shape
    x2d = x.reshape(batch * seq, D_MODEL)
    cos_d, sin_d, cos_m, sin_m = _rope_tables(seq)

    q, k, v = _mla_pre(x2d, seq, attn_g, wm, qa_g, kva_g, wq, wk, wv, cos_m, sin_m)
    qkv_d = _dil_proj(x2d, batch, seq, attn_g, w_dil, cos_d, sin_d)
    o_a = _mla_attn(q.reshape(batch, seq, -1), k.reshape(batch, seq, -1), v.reshape(batch, seq, -1))
    dil = []
    for qkv, (window, _) in zip(qkv_d, DIL_PATTERNS):
        o, lse = _dil_attn(qkv, window)
        dil.append((o, lse.reshape(batch * seq, LANES)))
    x1 = _mix_out(x2d, seq, o_a.reshape(batch * seq, MLA_WIDTH), dil, mla_g, dil_g, wo)
    y = _mlp(x1, mlp_g, wu, wd, final_g)
    return y.reshape(batch, seq, D_MODEL)


def kernel(x_prompt, x_sample, attn_norm_g, w_in, q_a_norm_g, w_q_b, kv_a_norm_g, w_kv_b,
           mla_out_norm_g, dil_out_norm_g, w_o, mlp_norm_g, w_up, w_down, final_norm_g):
    assert w_in.shape[0] == 1, "single-layer block"
    weights = _prepare_weights(w_in[0], w_q_b[0], w_kv_b[0], w_o[0], w_up[0], w_down[0])
    gains = (attn_norm_g[0][None], q_a_norm_g[0][None], kv_a_norm_g[0][None],
             mla_out_norm_g[0][None], dil_out_norm_g[0][None], mlp_norm_g[0][None],
             final_norm_g[None])
    return (_trunk(x_prompt, gains, weights), _trunk(x_sample, gains, weights))
```

```python
import functools
import math

import numpy as np
import jax
import jax.numpy as jnp
from jax import lax
from jax.experimental import pallas as pl
from jax.experimental.pallas import tpu as pltpu

D_MODEL = 2048
EPS = 1e-6
ROPE_THETA = 10000.0
MLA_HEADS = 8
Q_LORA = 512
KV_LORA = 512
NOPE_DIM = 128
ROPE_DIM = 64
V_DIM = 128
MLA_WIDTH = MLA_HEADS * V_DIM
DIL_HEADS = 8
DIL_HD = 128
DIL_WIDTH = DIL_HEADS * DIL_HD
DIL_PATTERNS = ((128, 1), (512, 4), (2048, 16))
D_FF = 4 * D_MODEL

LANES = 128
VMEM_LIMIT_BYTES = 56 * 1024 * 1024
MXU_DIM = 256
ROW_CHUNK = 128

MLA_QK = NOPE_DIM + LANES
ROPE_HALF = ROPE_DIM // 2
MASK_VALUE = -1e30

_NT = (((1,), (1,)), ((), ()))


def _rms(x, g):
    return x * lax.rsqrt(jnp.mean(x * x, axis=-1, keepdims=True) + EPS) * g


def _rope128(x, cos, sin):
    return x * cos + pltpu.roll(x, LANES // 2, 1) * sin


def _bf16_dot(a, b):
    return jnp.dot(a, b, preferred_element_type=jnp.float32)


def _params(*sem):
    return pltpu.CompilerParams(dimension_semantics=sem, vmem_limit_bytes=VMEM_LIMIT_BYTES)


def _mla_pre_kernel(x_ref, g_ref, wm_ref, qg_ref, kvg_ref, wq_ref, wk_ref, wv_ref,
                    cos_ref, sin_ref, q_out, k_out, v_out):
    h = _rms(x_ref[...], g_ref[...]).astype(jnp.bfloat16)
    a = _bf16_dot(h, wm_ref[...])
    qn = _rms(a[:, :Q_LORA], qg_ref[...]).astype(jnp.bfloat16)
    kvn = _rms(a[:, Q_LORA:Q_LORA + KV_LORA], kvg_ref[...]).astype(jnp.bfloat16)
    cos, sin = cos_ref[...], sin_ref[...]
    k_rope = _rope128(a[:, Q_LORA + KV_LORA:], cos, sin).astype(jnp.bfloat16)
    q = _bf16_dot(qn, wq_ref[...])
    k_nope = _bf16_dot(kvn, wk_ref[...])
    for hd in range(MLA_HEADS):
        lo = hd * MLA_QK
        q_out[:, lo:lo + NOPE_DIM] = q[:, lo:lo + NOPE_DIM].astype(jnp.bfloat16)
        q_out[:, lo + NOPE_DIM:lo + MLA_QK] = _rope128(
            q[:, lo + NOPE_DIM:lo + MLA_QK], cos, sin).astype(jnp.bfloat16)
        k_out[:, lo:lo + NOPE_DIM] = k_nope[:, hd * NOPE_DIM:(hd + 1) * NOPE_DIM].astype(jnp.bfloat16)
        k_out[:, lo + NOPE_DIM:lo + MLA_QK] = k_rope
    v_out[...] = _bf16_dot(kvn, wv_ref[...]).astype(jnp.bfloat16)


def _resident(shape):
    return pl.BlockSpec(shape, lambda *_: (0,) * len(shape), pipeline_mode=pl.Buffered(1))


def _mla_pre(x2d, seq, g, wm, qg, kvg, wq, wk, wv, cos_m, sin_m, tm=512):
    t = x2d.shape[0]
    pos_blocks = seq // tm
    const = lambda i: (0, 0)
    row = lambda i: (i, 0)
    pos = lambda i: (i % pos_blocks, 0)
    return pl.pallas_call(
        _mla_pre_kernel,
        grid=(t // tm,),
        in_specs=[
            pl.BlockSpec((tm, D_MODEL), row),
            pl.BlockSpec((1, D_MODEL), const),
            _resident(wm.shape),
            pl.BlockSpec((1, Q_LORA), const),
            pl.BlockSpec((1, KV_LORA), const),
            _resident(wq.shape),
            _resident(wk.shape),
            _resident(wv.shape),
            pl.BlockSpec((tm, LANES), pos),
            pl.BlockSpec((tm, LANES), pos),
        ],
        out_specs=[
            pl.BlockSpec((tm, MLA_HEADS * MLA_QK), row),
            pl.BlockSpec((tm, MLA_HEADS * MLA_QK), row),
            pl.BlockSpec((tm, MLA_WIDTH), row),
        ],
        out_shape=[
            jax.ShapeDtypeStruct((t, MLA_HEADS * MLA_QK), jnp.bfloat16),
            jax.ShapeDtypeStruct((t, MLA_HEADS * MLA_QK), jnp.bfloat16),
            jax.ShapeDtypeStruct((t, MLA_WIDTH), jnp.bfloat16),
        ],
        compiler_params=_params("parallel"),
        name="mla_pre",
    )(x2d, g, wm, qg, kvg, wq, wk, wv, cos_m, sin_m)


def _dil_proj_kernel(x_ref, g_ref, w_ref, cos_ref, sin_ref, *refs):
    n = len(DIL_PATTERNS)
    out_refs, planes = refs[:n], refs[n:]
    dils = [d for _, d in DIL_PATTERNS]
    tm = x_ref.shape[0]
    h = _rms(x_ref[...], g_ref[...]).astype(jnp.bfloat16)
    cos, sin = cos_ref[...], sin_ref[...]
    for which in range(3):
        y = _bf16_dot(h, w_ref[:, which * DIL_WIDTH:(which + 1) * DIL_WIDTH])
        for hd in range(DIL_HEADS):
            lanes = slice(hd * DIL_HD, (hd + 1) * DIL_HD)
            yh = y[:, lanes]
            if which < 2:
                yh = _rope128(yh, cos, sin)
            out_refs[0][which, 0, 0, :, lanes] = yh.astype(jnp.bfloat16)
            planes[0][hd] = yh
            for lvl in range(1, n):
                d_prev, d = dils[lvl - 1], dils[lvl]
                ratio, rows_prev, rows = d // d_prev, tm // d_prev, tm // d
                for r_prev in range(d_prev):
                    for sub in range(ratio):
                        c = planes[lvl - 1][hd, pl.ds(r_prev * rows_prev + sub, rows, stride=ratio), :]
                        cls = d_prev * sub + r_prev
                        out_refs[lvl][which, 0, cls, :, lanes] = c.astype(jnp.bfloat16)
                        if lvl + 1 < n:
                            planes[lvl][hd, cls * rows:(cls + 1) * rows, :] = c


def _dil_proj(x2d, batch, seq, g, w_dil, cos_d, sin_d, tm=512):
    t = x2d.shape[0]
    tiles = seq // tm
    dils = [d for _, d in DIL_PATTERNS]
    assert dils[0] == 1 and all(b % a == 0 for a, b in zip(dils, dils[1:]))
    const = lambda i: (0, 0)
    pos = lambda i: (i % tiles, 0)
    return pl.pallas_call(
        _dil_proj_kernel,
        grid=(t // tm,),
        in_specs=[
            pl.BlockSpec((tm, D_MODEL), lambda i: (i, 0)),
            pl.BlockSpec((1, D_MODEL), const),
            _resident(w_dil.shape),
            pl.BlockSpec((tm, LANES), pos),
            pl.BlockSpec((tm, LANES), pos),
        ],
        out_specs=[
            pl.BlockSpec((3, 1, d, tm // d, DIL_WIDTH), lambda i: (0, i // tiles, 0, i % tiles, 0))
            for d in dils
        ],
        out_shape=[jax.ShapeDtypeStruct((3, batch, d, seq // d, DIL_WIDTH), jnp.bfloat16) for d in dils],
        scratch_shapes=[pltpu.VMEM((DIL_HEADS, tm, DIL_HD), jnp.float32) for _ in dils[:-1]],
        compiler_params=_params("parallel"),
        name="dil_proj",
    )(x2d, g, w_dil, cos_d, sin_d)


def _mla_attn_kernel(q_ref, k_ref, v_ref, o_ref, vext_scr, *, chunk):
    exp2_scale = math.log2(math.e) / math.sqrt(NOPE_DIM + ROPE_DIM)

    heads = vext_scr.shape[0]

    @pl.when(pl.program_id(2) == 0)
    def _():
        for hd in range(heads):
            vext_scr[hd, :, :V_DIM] = v_ref[0, :, hd * V_DIM:(hd + 1) * V_DIM]
            vext_scr[hd, :, V_DIM:] = jnp.ones((vext_scr.shape[1], V_DIM), jnp.bfloat16)

    for hd in range(heads):
        q = q_ref[0, :, hd * MLA_QK:(hd + 1) * MLA_QK]
        m = acc = None
        for c in range(k_ref.shape[1] // chunk):
            rows = slice(c * chunk, (c + 1) * chunk)
            s = lax.dot_general(q, k_ref[0, rows, hd * MLA_QK:(hd + 1) * MLA_QK], _NT,
                                preferred_element_type=jnp.float32)
            m_c = jnp.max(s, axis=-1, keepdims=True)
            m_new = m_c if m is None else jnp.maximum(m, m_c)
            p = jnp.exp2((s - m_new) * exp2_scale).astype(jnp.bfloat16)
            pv = _bf16_dot(p, vext_scr[hd, rows, :])
            acc = pv if m is None else acc * jnp.exp2((m - m_new) * exp2_scale) + pv
            m = m_new
        o = acc[:, :V_DIM] / acc[:, V_DIM:V_DIM + 1]
        o_ref[0, :, hd * V_DIM:(hd + 1) * V_DIM] = o.astype(o_ref.dtype)


def _mla_attn(q, k, v, tq=1024, chunk=512, heads=2):
    b, s, _ = q.shape
    return pl.pallas_call(
        functools.partial(_mla_attn_kernel, chunk=chunk),
        grid=(b, MLA_HEADS // heads, s // tq),
        in_specs=[
            pl.BlockSpec((1, tq, heads * MLA_QK), lambda bi, h, i: (bi, i, h)),
            pl.BlockSpec((1, s, heads * MLA_QK), lambda bi, h, i: (bi, 0, h)),
            pl.BlockSpec((1, s, heads * V_DIM), lambda bi, h, i: (bi, 0, h)),
        ],
        out_specs=pl.BlockSpec((1, tq, heads * V_DIM), lambda bi, h, i: (bi, i, h)),
        out_shape=jax.ShapeDtypeStruct((b, s, MLA_WIDTH), jnp.bfloat16),
        scratch_shapes=[pltpu.VMEM((heads, s, 2 * V_DIM), jnp.bfloat16)],
        compiler_params=_params("parallel", "parallel", "arbitrary"),
        name="mla_attn",
    )(q, k, v)


def _dil_attn_kernel(q_ref, kp_ref, kc_ref, kn_ref, vp_ref, vc_ref, vn_ref, o_ref, lse_ref, o_scr, lse_scr,
                     *, tq, sub, half, class_len, dilation):
    i = pl.program_id(1)
    group = pl.program_id(2)
    classes = q_ref.shape[2]
    nk = sub + 2 * half
    inv_sqrt = 1.0 / math.sqrt(DIL_HD)
    lane = lax.broadcasted_iota(jnp.int32, (sub, LANES), 1)
    rel = lax.broadcasted_iota(jnp.int32, (sub, nk), 1) - half - lax.broadcasted_iota(jnp.int32, (sub, nk), 0)
    in_band = jnp.where(rel <= half, jnp.where(rel >= -half, 1, 0), 0)
    ones = jnp.ones((nk, DIL_HD), jnp.bfloat16)
    for cl in range(classes):
        k = jnp.concatenate([kp_ref[0, 0, cl], kc_ref[0, 0, cl], kn_ref[0, 0, cl]], axis=0)
        v = jnp.concatenate([vp_ref[0, 0, cl], vc_ref[0, 0, cl], vn_ref[0, 0, cl]], axis=0)
        for st in range(tq // sub):
            rows = slice(st * sub, (st + 1) * sub)
            q = q_ref[0, 0, cl, rows, :]
            k_pos = i * tq + st * sub - half + lax.broadcasted_iota(jnp.int32, (sub, nk), 1)
            valid = jnp.where(k_pos >= 0, jnp.where(k_pos < class_len, in_band, 0), 0)
            bias = jnp.where(valid > 0, 0.0, MASK_VALUE).astype(jnp.float32)
            lse_tile = jnp.zeros((sub, LANES), jnp.float32)
            for hd in range(DIL_HEADS):
                lanes = slice(hd * DIL_HD, (hd + 1) * DIL_HD)
                kh = k[st * sub:st * sub + nk, lanes]
                vh = v[st * sub:st * sub + nk, lanes]
                s = lax.dot_general(q[:, lanes], kh, _NT, preferred_element_type=jnp.float32) + bias
                m = jnp.max(s, axis=-1, keepdims=True)
                e = jnp.exp2((s - m) * (inv_sqrt * math.log2(math.e))).astype(jnp.bfloat16)
                pv = _bf16_dot(e, jnp.concatenate([vh, ones], axis=1))
                den = pv[:, DIL_HD:]
                o_scr[hd, cl, rows, :] = pv[:, :DIL_HD] / den
                lse_tile = jnp.where(lane == hd, m * inv_sqrt + jnp.log(den), lse_tile)
            lse_scr[cl, rows, :] = lse_tile

    for gs in range(dilation // classes):
        @pl.when(group == gs)
        def _():
            for cl in range(classes):
                r = gs * classes + cl
                rows = pl.ds(r, tq, stride=dilation) if dilation > 1 else slice(None)
                for hd in range(DIL_HEADS):
                    o_ref[0, hd, rows, :] = o_scr[hd, cl]
                lse_ref[0, rows, :] = lse_scr[cl]


def _dil_attn(qkv, window, rows_per_step=512, sub=128):
    _, batch, dilation, class_len, _ = qkv.shape
    half = window // (2 * dilation)
    tq = min(rows_per_step, class_len)
    classes = min(dilation, rows_per_step // tq)
    sub = min(sub, tq)
    nb = tq // half
    last = class_len // half - 1

    def spec(which, rows, row_block):
        return pl.BlockSpec((1, 1, classes, rows, DIL_WIDTH), lambda b, i, g: (which, b, g, row_block(i), 0))

    cur = lambda i: i
    prev = lambda i: jnp.maximum(i * nb - 1, 0)
    nxt = lambda i: jnp.minimum((i + 1) * nb, last)
    return pl.pallas_call(
        functools.partial(_dil_attn_kernel, tq=tq, sub=sub, half=half, class_len=class_len,
                          dilation=dilation),
        grid=(batch, class_len // tq, dilation // classes),
        in_specs=[
            spec(0, tq, cur),
            spec(1, half, prev), spec(1, tq, cur), spec(1, half, nxt),
            spec(2, half, prev), spec(2, tq, cur), spec(2, half, nxt),
        ],
        out_specs=[
            pl.BlockSpec((1, DIL_HEADS, tq * dilation, DIL_HD), lambda b, i, g: (b, 0, i, 0)),
            pl.BlockSpec((1, tq * dilation, LANES), lambda b, i, g: (b, i, 0)),
        ],
        out_shape=[
            jax.ShapeDtypeStruct((batch, DIL_HEADS, class_len * dilation, DIL_HD), jnp.float32),
            jax.ShapeDtypeStruct((batch, class_len * dilation, LANES), jnp.float32),
        ],
        scratch_shapes=[
            pltpu.VMEM((DIL_HEADS, classes, tq, DIL_HD), jnp.float32),
            pltpu.VMEM((classes, tq, LANES), jnp.float32),
        ],
        compiler_params=_params("parallel", "parallel", "arbitrary"),
        name=f"dil_attn_d{dilation}",
    )(qkv, qkv, qkv, qkv, qkv, qkv, qkv)


def _mix_out_kernel(x_ref, oa_ref, o1_ref, o2_ref, o3_ref, l1_ref, l2_ref, l3_ref,
                    ga_ref, gb_ref, wo_ref, y_ref):
    for c in range(x_ref.shape[0] // ROW_CHUNK):
        rows = slice(c * ROW_CHUNK, (c + 1) * ROW_CHUNK)
        l1, l2, l3 = l1_ref[rows, :], l2_ref[rows, :], l3_ref[rows, :]
        mx = jnp.maximum(jnp.maximum(l1, l2), l3)
        e1, e2, e3 = jnp.exp(l1 - mx), jnp.exp(l2 - mx), jnp.exp(l3 - mx)
        tot = e1 + e2 + e3
        w1, w2 = e1 / tot, e2 / tot
        parts = []
        for hd in range(DIL_HEADS):
            col = slice(hd, hd + 1)
            o3 = o3_ref[0, hd, rows, :]
            parts.append(o3 + w1[:, col] * (o1_ref[0, hd, rows, :] - o3)
                         + w2[:, col] * (o2_ref[0, hd, rows, :] - o3))
        o_b = jnp.concatenate(parts, axis=-1)
        mix_a = _rms(oa_ref[rows, :].astype(jnp.float32), ga_ref[...]).astype(jnp.bfloat16)
        mix_b = _rms(o_b, gb_ref[...]).astype(jnp.bfloat16)
        y = _bf16_dot(mix_a, wo_ref[:MLA_WIDTH, :]) + _bf16_dot(mix_b, wo_ref[MLA_WIDTH:, :])
        y_ref[rows, :] = x_ref[rows, :] + y


def _mix_out(x2d, seq, o_a, dil, ga, gb, wo, tm=512):
    t = x2d.shape[0]
    tiles = seq // tm
    row = lambda i: (i, 0)
    const = lambda i: (0, 0)
    heads = pl.BlockSpec((1, DIL_HEADS, tm, DIL_HD), lambda i: (i // tiles, 0, i % tiles, 0))
    (o1, l1), (o2, l2), (o3, l3) = dil
    return pl.pallas_call(
        _mix_out_kernel,
        grid=(t // tm,),
        in_specs=[
            pl.BlockSpec((tm, D_MODEL), row),
            pl.BlockSpec((tm, MLA_WIDTH), row),
            heads,
            heads,
            heads,
            pl.BlockSpec((tm, LANES), row),
            pl.BlockSpec((tm, LANES), row),
            pl.BlockSpec((tm, LANES), row),
            pl.BlockSpec((1, MLA_WIDTH), const),
            pl.BlockSpec((1, DIL_WIDTH), const),
            _resident(wo.shape),
        ],
        out_specs=pl.BlockSpec((tm, D_MODEL), row),
        out_shape=jax.ShapeDtypeStruct((t, D_MODEL), jnp.float32),
        compiler_params=_params("parallel"),
        name="mix_out",
    )(x2d, o_a, o1, o2, o3, l1, l2, l3, ga, gb, wo)


def _mlp_kernel(x_ref, g_ref, wu_ref, wd_ref, gf_ref, y_ref, h_scr, acc_scr):
    k = pl.program_id(1)

    @pl.when(k == 0)
    def _():
        x = x_ref[...]
        h_scr[...] = _rms(x, g_ref[...]).astype(jnp.bfloat16)
        acc_scr[...] = x

    u = _bf16_dot(h_scr[...], wu_ref[...])
    a = jnp.square(jnp.maximum(u, 0.0)).astype(jnp.bfloat16)
    acc_scr[...] += _bf16_dot(a, wd_ref[...])

    @pl.when(k == pl.num_programs(1) - 1)
    def _():
        y_ref[...] = _rms(acc_scr[...], gf_ref[...])


def _mlp(x2d, g, wu, wd, gf, tm=512, tf=1024):
    t = x2d.shape[0]
    return pl.pallas_call(
        _mlp_kernel,
        grid=(t // tm, D_FF // tf),
        in_specs=[
            pl.BlockSpec((tm, D_MODEL), lambda i, k: (i, 0)),
            pl.BlockSpec((1, D_MODEL), lambda i, k: (0, 0)),
            pl.BlockSpec((D_MODEL, tf), lambda i, k: (0, k)),
            pl.BlockSpec((tf, D_MODEL), lambda i, k: (k, 0)),
            pl.BlockSpec((1, D_MODEL), lambda i, k: (0, 0)),
        ],
        out_specs=pl.BlockSpec((tm, D_MODEL), lambda i, k: (i, 0)),
        out_shape=jax.ShapeDtypeStruct((t, D_MODEL), jnp.float32),
        scratch_shapes=[
            pltpu.VMEM((tm, D_MODEL), jnp.bfloat16),
            pltpu.VMEM((tm, D_MODEL), jnp.float32),
        ],
        compiler_params=_params("parallel", "arbitrary"),
        name="mlp",
    )(x2d, g, wu, wd, gf)


def _rope_tables(seq):
    pos = np.arange(seq, dtype=np.float64)

    def cos_sin(dim):
        inv = 1.0 / (ROPE_THETA ** (np.arange(0, dim, 2, dtype=np.float64) / dim))
        ang = pos[:, None] * inv[None, :]
        return np.cos(ang), np.sin(ang)

    c, s = cos_sin(DIL_HD)
    cos_d = np.concatenate([c, c], axis=-1)
    sin_d = np.concatenate([-s, s], axis=-1)
    c, s = cos_sin(ROPE_DIM)
    z = np.zeros_like(c)
    cos_m = np.concatenate([c, z, c, z], axis=-1)
    sin_m = np.concatenate([-s, z, s, z], axis=-1)
    return tuple(jnp.asarray(t, dtype=jnp.float32) for t in (cos_d, sin_d, cos_m, sin_m))


def _spread_rope_cols(w):
    z = jnp.zeros(w.shape[:-1] + (ROPE_HALF,), w.dtype)
    return jnp.concatenate([w[..., :ROPE_HALF], z, w[..., ROPE_HALF:], z], axis=-1)


def _prepare_weights(w_in, w_q_b, w_kv_b, w_o, w_up, w_down):
    bf = jnp.bfloat16
    n_lora = Q_LORA + KV_LORA
    wm = jnp.concatenate([w_in[:, :n_lora], _spread_rope_cols(w_in[:, n_lora:n_lora + ROPE_DIM])],
                         axis=-1).astype(bf)
    w_dil = w_in[:, n_lora + ROPE_DIM:].astype(bf)
    wq = w_q_b.reshape(Q_LORA, MLA_HEADS, NOPE_DIM + ROPE_DIM)
    wq = jnp.concatenate([wq[..., :NOPE_DIM], _spread_rope_cols(wq[..., NOPE_DIM:])], axis=-1)
    wq = wq.reshape(Q_LORA, MLA_HEADS * MLA_QK).astype(bf)
    wkv = w_kv_b.reshape(KV_LORA, MLA_HEADS, NOPE_DIM + V_DIM)
    wk = wkv[..., :NOPE_DIM].reshape(KV_LORA, MLA_HEADS * NOPE_DIM).astype(bf)
    wv = wkv[..., NOPE_DIM:].reshape(KV_LORA, MLA_WIDTH).astype(bf)
    return wm, w_dil, wq, wk, wv, w_o.astype(bf), w_up.astype(bf), w_down.astype(bf)


def _trunk(x, gains, weights):
    attn_g, qa_g, kva_g, mla_g, dil_g, mlp_g, final_g = gains
    wm, w_dil, wq, wk, wv, wo, wu, wd = weights
    batch, seq, _ = x.shape
    x2d = x.reshape(batch * seq, D_MODEL)
    cos_d, sin_d, cos_m, sin_m = _rope_tables(seq)

    q, k, v = _mla_pre(x2d, seq, attn_g, wm, qa_g, kva_g, wq, wk, wv, cos_m, sin_m)
    qkv_d = _dil_proj(x2d, batch, seq, attn_g, w_dil, cos_d, sin_d)
    o_a = _mla_attn(q.reshape(batch, seq, -1), k.reshape(batch, seq, -1), v.reshape(batch, seq, -1))
    dil = []
    for qkv, (window, _) in zip(qkv_d, DIL_PATTERNS):
        o, lse = _dil_attn(qkv, window)
        dil.append((o, lse.reshape(batch * seq, LANES)))
    x1 = _mix_out(x2d, seq, o_a.reshape(batch * seq, MLA_WIDTH), dil, mla_g, dil_g, wo)
    y = _mlp(x1, mlp_g, wu, wd, final_g)
    return y.reshape(batch, seq, D_MODEL)


def kernel(x_prompt, x_sample, attn_norm_g, w_in, q_a_norm_g, w_q_b, kv_a_norm_g, w_kv_b,
           mla_out_norm_g, dil_out_norm_g, w_o, mlp_norm_g, w_up, w_down, final_norm_g):
    assert w_in.shape[0] == 1, "single-layer block"
    weights = _prepare_weights(w_in[0], w_q_b[0], w_kv_b[0], w_o[0], w_up[0], w_down[0])
    gains = (attn_norm_g[0][None], q_a_norm_g[0][None], kv_a_norm_g[0][None],
             mla_out_norm_g[0][None], dil_out_norm_g[0][None], mlp_norm_g[0][None],
             final_norm_g[None])
    return (_trunk(x_prompt, gains, weights), _trunk(x_sample, gains, weights))
```

```python
import functools
import math

import numpy as np
import jax
import jax.numpy as jnp
from jax import lax
from jax.experimental import pallas as pl
from jax.experimental.pallas import tpu as pltpu

D_MODEL = 2048
EPS = 1e-6
ROPE_THETA = 10000.0
MLA_HEADS = 8
Q_LORA = 512
KV_LORA = 512
NOPE_DIM = 128
ROPE_DIM = 64
V_DIM = 128
MLA_WIDTH = MLA_HEADS * V_DIM
DIL_HEADS = 8
DIL_HD = 128
DIL_WIDTH = DIL_HEADS * DIL_HD
DIL_PATTERNS = ((128, 1), (512, 4), (2048, 16))
D_FF = 4 * D_MODEL

LANES = 128
VMEM_LIMIT_BYTES = 56 * 1024 * 1024
MXU_DIM = 256
ROW_CHUNK = 128

MLA_QK = NOPE_DIM + LANES
ROPE_HALF = ROPE_DIM // 2
MASK_VALUE = -1e30

_NT = (((1,), (1,)), ((), ()))


def _rms(x, g):
    return x * lax.rsqrt(jnp.mean(x * x, axis=-1, keepdims=True) + EPS) * g


def _rope128(x, cos, sin):
    return x * cos + pltpu.roll(x, LANES // 2, 1) * sin


def _bf16_dot(a, b):
    return jnp.dot(a, b, preferred_element_type=jnp.float32)


def _params(*sem):
    return pltpu.CompilerParams(dimension_semantics=sem, vmem_limit_bytes=VMEM_LIMIT_BYTES)


def _mla_pre_kernel(x_ref, g_ref, wm_ref, qg_ref, kvg_ref, wq_ref, wk_ref, wv_ref,
                    cos_ref, sin_ref, q_out, k_out, v_out):
    h = _rms(x_ref[...], g_ref[...]).astype(jnp.bfloat16)
    a = _bf16_dot(h, wm_ref[...])
    qn = _rms(a[:, :Q_LORA], qg_ref[...]).astype(jnp.bfloat16)
    kvn = _rms(a[:, Q_LORA:Q_LORA + KV_LORA], kvg_ref[...]).astype(jnp.bfloat16)
    cos, sin = cos_ref[...], sin_ref[...]
    k_rope = _rope128(a[:, Q_LORA + KV_LORA:], cos, sin).astype(jnp.bfloat16)
    q = _bf16_dot(qn, wq_ref[...])
    k_nope = _bf16_dot(kvn, wk_ref[...])
    for hd in range(MLA_HEADS):
        lo = hd * MLA_QK
        q_out[:, lo:lo + NOPE_DIM] = q[:, lo:lo + NOPE_DIM].astype(jnp.bfloat16)
        q_out[:, lo + NOPE_DIM:lo + MLA_QK] = _rope128(
            q[:, lo + NOPE_DIM:lo + MLA_QK], cos, sin).astype(jnp.bfloat16)
        k_out[:, lo:lo + NOPE_DIM] = k_nope[:, hd * NOPE_DIM:(hd + 1) * NOPE_DIM].astype(jnp.bfloat16)
        k_out[:, lo + NOPE_DIM:lo + MLA_QK] = k_rope
    v_out[...] = _bf16_dot(kvn, wv_ref[...]).astype(jnp.bfloat16)


def _resident(shape):
    return pl.BlockSpec(shape, lambda *_: (0,) * len(shape), pipeline_mode=pl.Buffered(1))


def _mla_pre(x2d, seq, g, wm, qg, kvg, wq, wk, wv, cos_m, sin_m, tm=512):
    t = x2d.shape[0]
    pos_blocks = seq // tm
    const = lambda i: (0, 0)
    row = lambda i: (i, 0)
    pos = lambda i: (i % pos_blocks, 0)
    return pl.pallas_call(
        _mla_pre_kernel,
        grid=(t // tm,),
        in_specs=[
            pl.BlockSpec((tm, D_MODEL), row),
            pl.BlockSpec((1, D_MODEL), const),
            _resident(wm.shape),
            pl.BlockSpec((1, Q_LORA), const),
            pl.BlockSpec((1, KV_LORA), const),
            _resident(wq.shape),
            _resident(wk.shape),
            _resident(wv.shape),
            pl.BlockSpec((tm, LANES), pos),
            pl.BlockSpec((tm, LANES), pos),
        ],
        out_specs=[
            pl.BlockSpec((tm, MLA_HEADS * MLA_QK), row),
            pl.BlockSpec((tm, MLA_HEADS * MLA_QK), row),
            pl.BlockSpec((tm, MLA_WIDTH), row),
        ],
        out_shape=[
            jax.ShapeDtypeStruct((t, MLA_HEADS * MLA_QK), jnp.bfloat16),
            jax.ShapeDtypeStruct((t, MLA_HEADS * MLA_QK), jnp.bfloat16),
            jax.ShapeDtypeStruct((t, MLA_WIDTH), jnp.bfloat16),
        ],
        compiler_params=_params("parallel"),
        name="mla_pre",
    )(x2d, g, wm, qg, kvg, wq, wk, wv, cos_m, sin_m)


def _dil_proj_kernel(x_ref, g_ref, w_ref, cos_ref, sin_ref, *refs):
    n = len(DIL_PATTERNS)
    out_refs, planes = refs[:n], refs[n:]
    dils = [d for _, d in DIL_PATTERNS]
    tm = x_ref.shape[0]
    h = _rms(x_ref[...], g_ref[...]).astype(jnp.bfloat16)
    cos, sin = cos_ref[...], sin_ref[...]
    for which in range(3):
        y = _bf16_dot(h, w_ref[:, which * DIL_WIDTH:(which + 1) * DIL_WIDTH])
        for hd in range(DIL_HEADS):
            lanes = slice(hd * DIL_HD, (hd + 1) * DIL_HD)
            yh = y[:, lanes]
            if which < 2:
                yh = _rope128(yh, cos, sin)
            out_refs[0][which, 0, 0, :, lanes] = yh.astype(jnp.bfloat16)
            planes[0][hd] = yh
            for lvl in range(1, n):
                d_prev, d = dils[lvl - 1], dils[lvl]
                ratio, rows_prev, rows = d // d_prev, tm // d_prev, tm // d
                for r_prev in range(d_prev):
                    for sub in range(ratio):
                        c = planes[lvl - 1][hd, pl.ds(r_prev * rows_prev + sub, rows, stride=ratio), :]
                        cls = d_prev * sub + r_prev
                        out_refs[lvl][which, 0, cls, :, lanes] = c.astype(jnp.bfloat16)
                        if lvl + 1 < n:
                            planes[lvl][hd, cls * rows:(cls + 1) * rows, :] = c


def _dil_proj(x2d, batch, seq, g, w_dil, cos_d, sin_d, tm=512):
    t = x2d.shape[0]
    tiles = seq // tm
    dils = [d for _, d in DIL_PATTERNS]
    assert dils[0] == 1 and all(b % a == 0 for a, b in zip(dils, dils[1:]))
    const = lambda i: (0, 0)
    pos = lambda i: (i % tiles, 0)
    return pl.pallas_call(
        _dil_proj_kernel,
        grid=(t // tm,),
        in_specs=[
            pl.BlockSpec((tm, D_MODEL), lambda i: (i, 0)),
            pl.BlockSpec((1, D_MODEL), const),
            _resident(w_dil.shape),
            pl.BlockSpec((tm, LANES), pos),
            pl.BlockSpec((tm, LANES), pos),
        ],
        out_specs=[
            pl.BlockSpec((3, 1, d, tm // d, DIL_WIDTH), lambda i: (0, i // tiles, 0, i % tiles, 0))
            for d in dils
        ],
        out_shape=[jax.ShapeDtypeStruct((3, batch, d, seq // d, DIL_WIDTH), jnp.bfloat16) for d in dils],
        scratch_shapes=[pltpu.VMEM((DIL_HEADS, tm, DIL_HD), jnp.float32) for _ in dils[:-1]],
        compiler_params=_params("parallel"),
        name="dil_proj",
    )(x2d, g, w_dil, cos_d, sin_d)


def _mla_attn_kernel(q_ref, k_ref, v_ref, o_ref, vext_scr, *, chunk):
    exp2_scale = math.log2(math.e) / math.sqrt(NOPE_DIM + ROPE_DIM)

    heads = vext_scr.shape[0]

    @pl.when(pl.program_id(2) == 0)
    def _():
        for hd in range(heads):
            vext_scr[hd, :, :V_DIM] = v_ref[0, :, hd * V_DIM:(hd + 1) * V_DIM]
            vext_scr[hd, :, V_DIM:] = jnp.ones((vext_scr.shape[1], V_DIM), jnp.bfloat16)

    for hd in range(heads):
        q = q_ref[0, :, hd * MLA_QK:(hd + 1) * MLA_QK]
        m = acc = None
        for c in range(k_ref.shape[1] // chunk):
            rows = slice(c * chunk, (c + 1) * chunk)
            s = lax.dot_general(q, k_ref[0, rows, hd * MLA_QK:(hd + 1) * MLA_QK], _NT,
                                preferred_element_type=jnp.float32)
            m_c = jnp.max(s, axis=-1, keepdims=True)
            m_new = m_c if m is None else jnp.maximum(m, m_c)
            p = jnp.exp2((s - m_new) * exp2_scale).astype(jnp.bfloat16)
            pv = _bf16_dot(p, vext_scr[hd, rows, :])
            acc = pv if m is None else acc * jnp.exp2((m - m_new) * exp2_scale) + pv
            m = m_new
        o = acc[:, :V_DIM] / acc[:, V_DIM:V_DIM + 1]
        o_ref[0, :, hd * V_DIM:(hd + 1) * V_DIM] = o.astype(o_ref.dtype)


def _mla_attn(q, k, v, tq=1024, chunk=512, heads=2):
    b, s, _ = q.shape
    return pl.pallas_call(
        functools.partial(_mla_attn_kernel, chunk=chunk),
        grid=(b, MLA_HEADS // heads, s // tq),
        in_specs=[
            pl.BlockSpec((1, tq, heads * MLA_QK), lambda bi, h, i: (bi, i, h)),
            pl.BlockSpec((1, s, heads * MLA_QK), lambda bi, h, i: (bi, 0, h)),
            pl.BlockSpec((1, s, heads * V_DIM), lambda bi, h, i: (bi, 0, h)),
        ],
        out_specs=pl.BlockSpec((1, tq, heads * V_DIM), lambda bi, h, i: (bi, i, h)),
        out_shape=jax.ShapeDtypeStruct((b, s, MLA_WIDTH), jnp.bfloat16),
        scratch_shapes=[pltpu.VMEM((heads, s, 2 * V_DIM), jnp.bfloat16)],
        compiler_params=_params("parallel", "parallel", "arbitrary"),
        name="mla_attn",
    )(q, k, v)


def _dil_attn_kernel(*refs, block, sub, seq):
    n = len(DIL_PATTERNS)
    in_refs, o_ref, o_nat, l_nat = refs[:7 * n], refs[7 * n], refs[7 * n + 1], refs[7 * n + 2]
    i = pl.program_id(1)
    heads = o_nat.shape[1]
    inv_sqrt = 1.0 / math.sqrt(DIL_HD)
    for p, (window, dilation) in enumerate(DIL_PATTERNS):
        q_ref, kp_ref, kc_ref, kn_ref, vp_ref, vc_ref, vn_ref = in_refs[7 * p:7 * p + 7]
        half = window // (2 * dilation)
        rows_cls = block // dilation
        sub_p = min(sub, rows_cls)
        nk = sub_p + 2 * half
        rel = (lax.broadcasted_iota(jnp.int32, (sub_p, nk), 1) - half
               - lax.broadcasted_iota(jnp.int32, (sub_p, nk), 0))
        in_band = jnp.where(rel <= half, jnp.where(rel >= -half, 1, 0), 0)
        ones = jnp.ones((nk, DIL_HD), jnp.bfloat16)
        biases = []
        for st in range(rows_cls // sub_p):
            k_pos = i * rows_cls + st * sub_p - half + lax.broadcasted_iota(jnp.int32, (sub_p, nk), 1)
            valid = jnp.where(k_pos >= 0, jnp.where(k_pos < seq // dilation, in_band, 0), 0)
            biases.append(jnp.where(valid > 0, 0.0, MASK_VALUE).astype(jnp.float32))
        for cl in range(dilation):
            k = jnp.concatenate([kp_ref[0, 0, cl], kc_ref[0, 0, cl], kn_ref[0, 0, cl]], axis=0)
            v = jnp.concatenate([vp_ref[0, 0, cl], vc_ref[0, 0, cl], vn_ref[0, 0, cl]], axis=0)
            for st in range(rows_cls // sub_p):
                q = q_ref[0, 0, cl, st * sub_p:(st + 1) * sub_p, :]
                if dilation > 1:
                    tok = pl.ds(cl + dilation * st * sub_p, sub_p, stride=dilation)
                else:
                    tok = slice(st * sub_p, (st + 1) * sub_p)
                for hd in range(heads):
                    lanes = slice(hd * DIL_HD, (hd + 1) * DIL_HD)
                    kh = k[st * sub_p:st * sub_p + nk, lanes]
                    vh = v[st * sub_p:st * sub_p + nk, lanes]
                    s = lax.dot_general(q[:, lanes], kh, _NT, preferred_element_type=jnp.float32) + biases[st]
                    m = jnp.max(s, axis=-1, keepdims=True)
                    e = jnp.exp2((s - m) * (inv_sqrt * math.log2(math.e))).astype(jnp.bfloat16)
                    pv = _bf16_dot(e, jnp.concatenate([vh, ones], axis=1))
                    den = pv[:, DIL_HD:]
                    o_nat[p, hd, tok, :] = pv[:, :DIL_HD] / den
                    l_nat[p, hd, tok, :] = m * inv_sqrt + jnp.log(den)

    for hd in range(heads):
        for c in range(block // ROW_CHUNK):
            rows = slice(c * ROW_CHUNK, (c + 1) * ROW_CHUNK)
            lses = [l_nat[p, hd, rows, :] for p in range(n)]
            mx = functools.reduce(jnp.maximum, lses)
            es = [jnp.exp(l - mx) for l in lses]
            inv_tot = 1.0 / functools.reduce(jnp.add, es)
            o_last = o_nat[n - 1, hd, rows, :]
            acc = o_last
            for p in range(n - 1):
                acc = acc + (es[p] * inv_tot) * (o_nat[p, hd, rows, :] - o_last)
            o_ref[0, rows, hd * DIL_HD:(hd + 1) * DIL_HD] = acc


def _dil_attn(qkv_by_pattern, block=2048, sub=128, heads=2):
    _, batch, _, seq, _ = qkv_by_pattern[0].shape
    block = min(block, seq)
    width = heads * DIL_HD
    args, in_specs = [], []
    for qkv, (window, dilation) in zip(qkv_by_pattern, DIL_PATTERNS):
        half = window // (2 * dilation)
        rows_cls = block // dilation
        per_half = rows_cls // half
        last = seq // dilation // half - 1

        def spec(which, rows, row_block, dilation=dilation):
            return pl.BlockSpec((1, 1, dilation, rows, width),
                                lambda b, i, g: (which, b, 0, row_block(i), g))

        cur = lambda i: i
        prev = lambda i, per_half=per_half: jnp.maximum(i * per_half - 1, 0)
        nxt = lambda i, per_half=per_half, last=last: jnp.minimum((i + 1) * per_half, last)
        in_specs += [
            spec(0, rows_cls, cur),
            spec(1, half, prev), spec(1, rows_cls, cur), spec(1, half, nxt),
            spec(2, half, prev), spec(2, rows_cls, cur), spec(2, half, nxt),
        ]
        args += [qkv] * 7
    n = len(DIL_PATTERNS)
    return pl.pallas_call(
        functools.partial(_dil_attn_kernel, block=block, sub=sub, seq=seq),
        grid=(batch, seq // block, DIL_HEADS // heads),
        in_specs=in_specs,
        out_specs=pl.BlockSpec((1, block, width), lambda b, i, g: (b, i, g)),
        out_shape=jax.ShapeDtypeStruct((batch, seq, DIL_WIDTH), jnp.float32),
        scratch_shapes=[
            pltpu.VMEM((n, heads, block, DIL_HD), jnp.float32),
            pltpu.VMEM((n, heads, block, DIL_HD), jnp.float32),
        ],
        compiler_params=_params("parallel", "parallel", "parallel"),
        name="dil_attn",
    )(*args)


def _mix_out_kernel(x_ref, oa_ref, ob_ref, ga_ref, gb_ref, wo_ref, y_ref):
    for c in range(x_ref.shape[0] // ROW_CHUNK):
        rows = slice(c * ROW_CHUNK, (c + 1) * ROW_CHUNK)
        mix_a = _rms(oa_ref[rows, :].astype(jnp.float32), ga_ref[...]).astype(jnp.bfloat16)
        mix_b = _rms(ob_ref[rows, :], gb_ref[...]).astype(jnp.bfloat16)
        y = _bf16_dot(mix_a, wo_ref[:MLA_WIDTH, :]) + _bf16_dot(mix_b, wo_ref[MLA_WIDTH:, :])
        y_ref[rows, :] = x_ref[rows, :] + y


def _mix_out(x2d, o_a, o_b, ga, gb, wo, tm=512):
    t = x2d.shape[0]
    row = lambda i: (i, 0)
    const = lambda i: (0, 0)
    return pl.pallas_call(
        _mix_out_kernel,
        grid=(t // tm,),
        in_specs=[
            pl.BlockSpec((tm, D_MODEL), row),
            pl.BlockSpec((tm, MLA_WIDTH), row),
            pl.BlockSpec((tm, DIL_WIDTH), row),
            pl.BlockSpec((1, MLA_WIDTH), const),
            pl.BlockSpec((1, DIL_WIDTH), const),
            _resident(wo.shape),
        ],
        out_specs=pl.BlockSpec((tm, D_MODEL), row),
        out_shape=jax.ShapeDtypeStruct((t, D_MODEL), jnp.float32),
        compiler_params=_params("parallel"),
        name="mix_out",
    )(x2d, o_a, o_b, ga, gb, wo)


def _mlp_kernel(x_ref, g_ref, wu_ref, wd_ref, gf_ref, y_ref, h_scr, acc_scr):
    k = pl.program_id(1)

    @pl.when(k == 0)
    def _():
        x = x_ref[...]
        h_scr[...] = _rms(x, g_ref[...]).astype(jnp.bfloat16)
        acc_scr[...] = x

    u = _bf16_dot(h_scr[...], wu_ref[...])
    a = jnp.square(jnp.maximum(u, 0.0)).astype(jnp.bfloat16)
    acc_scr[...] += _bf16_dot(a, wd_ref[...])

    @pl.when(k == pl.num_programs(1) - 1)
    def _():
        y_ref[...] = _rms(acc_scr[...], gf_ref[...])


def _mlp(x2d, g, wu, wd, gf, tm=512, tf=1024):
    t = x2d.shape[0]
    return pl.pallas_call(
        _mlp_kernel,
        grid=(t // tm, D_FF // tf),
        in_specs=[
            pl.BlockSpec((tm, D_MODEL), lambda i, k: (i, 0)),
            pl.BlockSpec((1, D_MODEL), lambda i, k: (0, 0)),
            pl.BlockSpec((D_MODEL, tf), lambda i, k: (0, k)),
            pl.BlockSpec((tf, D_MODEL), lambda i, k: (k, 0)),
            pl.BlockSpec((1, D_MODEL), lambda i, k: (0, 0)),
        ],
        out_specs=pl.BlockSpec((tm, D_MODEL), lambda i, k: (i, 0)),
        out_shape=jax.ShapeDtypeStruct((t, D_MODEL), jnp.float32),
        scratch_shapes=[
            pltpu.VMEM((tm, D_MODEL), jnp.bfloat16),
            pltpu.VMEM((tm, D_MODEL), jnp.float32),
        ],
        compiler_params=_params("parallel", "arbitrary"),
        name="mlp",
    )(x2d, g, wu, wd, gf)


def _rope_tables(seq):
    pos = np.arange(seq, dtype=np.float64)

    def cos_sin(dim):
        inv = 1.0 / (ROPE_THETA ** (np.arange(0, dim, 2, dtype=np.float64) / dim))
        ang = pos[:, None] * inv[None, :]
        return np.cos(ang), np.sin(ang)

    c, s = cos_sin(DIL_HD)
    cos_d = np.concatenate([c, c], axis=-1)
    sin_d = np.concatenate([-s, s], axis=-1)
    c, s = cos_sin(ROPE_DIM)
    z = np.zeros_like(c)
    cos_m = np.concatenate([c, z, c, z], axis=-1)
    sin_m = np.concatenate([-s, z, s, z], axis=-1)
    return tuple(jnp.asarray(t, dtype=jnp.float32) for t in (cos_d, sin_d, cos_m, sin_m))


def _spread_rope_cols(w):
    z = jnp.zeros(w.shape[:-1] + (ROPE_HALF,), w.dtype)
    return jnp.concatenate([w[..., :ROPE_HALF], z, w[..., ROPE_HALF:], z], axis=-1)


def _prepare_weights(w_in, w_q_b, w_kv_b, w_o, w_up, w_down):
    bf = jnp.bfloat16
    n_lora = Q_LORA + KV_LORA
    wm = jnp.concatenate([w_in[:, :n_lora], _spread_rope_cols(w_in[:, n_lora:n_lora + ROPE_DIM])],
                         axis=-1).astype(bf)
    w_dil = w_in[:, n_lora + ROPE_DIM:].astype(bf)
    wq = w_q_b.reshape(Q_LORA, MLA_HEADS, NOPE_DIM + ROPE_DIM)
    wq = jnp.concatenate([wq[..., :NOPE_DIM], _spread_rope_cols(wq[..., NOPE_DIM:])], axis=-1)
    wq = wq.reshape(Q_LORA, MLA_HEADS * MLA_QK).astype(bf)
    wkv = w_kv_b.reshape(KV_LORA, MLA_HEADS, NOPE_DIM + V_DIM)
    wk = wkv[..., :NOPE_DIM].reshape(KV_LORA, MLA_HEADS * NOPE_DIM).astype(bf)
    wv = wkv[..., NOPE_DIM:].reshape(KV_LORA, MLA_WIDTH).astype(bf)
    return wm, w_dil, wq, wk, wv, w_o.astype(bf), w_up.astype(bf), w_down.astype(bf)


def _trunk(x, gains, weights):
    attn_g, qa_g, kva_g, mla_g, dil_g, mlp_g, final_g = gains
    wm, w_dil, wq, wk, wv, wo, wu, wd = weights
    batch, seq, _ = x.shape
    x2d = x.reshape(batch * seq, D_MODEL)
    cos_d, sin_d, cos_m, sin_m = _rope_tables(seq)

    q, k, v = _mla_pre(x2d, seq, attn_g, wm, qa_g, kva_g, wq, wk, wv, cos_m, sin_m)
    qkv_d = _dil_proj(x2d, batch, seq, attn_g, w_dil, cos_d, sin_d)
    o_a = _mla_attn(q.reshape(batch, seq, -1), k.reshape(batch, seq, -1), v.reshape(batch, seq, -1))
    o_b = _dil_attn(qkv_d)
    x1 = _mix_out(x2d, o_a.reshape(batch * seq, MLA_WIDTH), o_b.reshape(batch * seq, DIL_WIDTH),
                  mla_g, dil_g, wo)
    y = _mlp(x1, mlp_g, wu, wd, final_g)
    return y.reshape(batch, seq, D_MODEL)


def kernel(x_prompt, x_sample, attn_norm_g, w_in, q_a_norm_g, w_q_b, kv_a_norm_g, w_kv_b,
           mla_out_norm_g, dil_out_norm_g, w_o, mlp_norm_g, w_up, w_down, final_norm_g):
    assert w_in.shape[0] == 1, "single-layer block"
    weights = _prepare_weights(w_in[0], w_q_b[0], w_kv_b[0], w_o[0], w_up[0], w_down[0])
    gains = (attn_norm_g[0][None], q_a_norm_g[0][None], kv_a_norm_g[0][None],
             mla_out_norm_g[0][None], dil_out_norm_g[0][None], mlp_norm_g[0][None],
             final_norm_g[None])
    return (_trunk(x_prompt, gains, weights), _trunk(x_sample, gains, weights))
```

```python
import functools
import math

import numpy as np
import jax
import jax.numpy as jnp
from jax import lax
from jax.experimental import pallas as pl
from jax.experimental.pallas import tpu as pltpu

D_MODEL = 2048
EPS = 1e-6
ROPE_THETA = 10000.0
MLA_HEADS = 8
Q_LORA = 512
KV_LORA = 512
NOPE_DIM = 128
ROPE_DIM = 64
V_DIM = 128
MLA_WIDTH = MLA_HEADS * V_DIM
DIL_HEADS = 8
DIL_HD = 128
DIL_WIDTH = DIL_HEADS * DIL_HD
DIL_PATTERNS = ((128, 1), (512, 4), (2048, 16))
D_FF = 4 * D_MODEL

LANES = 128
VMEM_LIMIT_BYTES = 56 * 1024 * 1024
MXU_DIM = 256
ROW_CHUNK = 128

MLA_QK = NOPE_DIM + LANES
ROPE_HALF = ROPE_DIM // 2
MASK_VALUE = -1e30

_NT = (((1,), (1,)), ((), ()))


def _rms(x, g):
    return x * lax.rsqrt(jnp.mean(x * x, axis=-1, keepdims=True) + EPS) * g


def _rope128(x, cos, sin):
    return x * cos + pltpu.roll(x, LANES // 2, 1) * sin


def _bf16_dot(a, b):
    return jnp.dot(a, b, preferred_element_type=jnp.float32)


def _params(*sem):
    return pltpu.CompilerParams(dimension_semantics=sem, vmem_limit_bytes=VMEM_LIMIT_BYTES)


def _mla_pre_kernel(x_ref, g_ref, wm_ref, qg_ref, kvg_ref, wq_ref, wk_ref, wv_ref,
                    cos_ref, sin_ref, q_out, k_out, v_out):
    h = _rms(x_ref[...], g_ref[...]).astype(jnp.bfloat16)
    a = _bf16_dot(h, wm_ref[...])
    qn = _rms(a[:, :Q_LORA], qg_ref[...]).astype(jnp.bfloat16)
    kvn = _rms(a[:, Q_LORA:Q_LORA + KV_LORA], kvg_ref[...]).astype(jnp.bfloat16)
    cos, sin = cos_ref[...], sin_ref[...]
    k_rope = _rope128(a[:, Q_LORA + KV_LORA:], cos, sin).astype(jnp.bfloat16)
    q = _bf16_dot(qn, wq_ref[...])
    k_nope = _bf16_dot(kvn, wk_ref[...])
    for hd in range(MLA_HEADS):
        lo = hd * MLA_QK
        q_out[:, lo:lo + NOPE_DIM] = q[:, lo:lo + NOPE_DIM].astype(jnp.bfloat16)
        q_out[:, lo + NOPE_DIM:lo + MLA_QK] = _rope128(
            q[:, lo + NOPE_DIM:lo + MLA_QK], cos, sin).astype(jnp.bfloat16)
        k_out[:, lo:lo + NOPE_DIM] = k_nope[:, hd * NOPE_DIM:(hd + 1) * NOPE_DIM].astype(jnp.bfloat16)
        k_out[:, lo + NOPE_DIM:lo + MLA_QK] = k_rope
    v_out[...] = _bf16_dot(kvn, wv_ref[...]).astype(jnp.bfloat16)


def _resident(shape):
    return pl.BlockSpec(shape, lambda *_: (0,) * len(shape), pipeline_mode=pl.Buffered(1))


def _mla_pre(x2d, seq, g, wm, qg, kvg, wq, wk, wv, cos_m, sin_m, tm=512):
    t = x2d.shape[0]
    pos_blocks = seq // tm
    const = lambda i: (0, 0)
    row = lambda i: (i, 0)
    pos = lambda i: (i % pos_blocks, 0)
    return pl.pallas_call(
        _mla_pre_kernel,
        grid=(t // tm,),
        in_specs=[
            pl.BlockSpec((tm, D_MODEL), row),
            pl.BlockSpec((1, D_MODEL), const),
            _resident(wm.shape),
            pl.BlockSpec((1, Q_LORA), const),
            pl.BlockSpec((1, KV_LORA), const),
            _resident(wq.shape),
            _resident(wk.shape),
            _resident(wv.shape),
            pl.BlockSpec((tm, LANES), pos),
            pl.BlockSpec((tm, LANES), pos),
        ],
        out_specs=[
            pl.BlockSpec((tm, MLA_HEADS * MLA_QK), row),
            pl.BlockSpec((tm, MLA_HEADS * MLA_QK), row),
            pl.BlockSpec((tm, MLA_WIDTH), row),
        ],
        out_shape=[
            jax.ShapeDtypeStruct((t, MLA_HEADS * MLA_QK), jnp.bfloat16),
            jax.ShapeDtypeStruct((t, MLA_HEADS * MLA_QK), jnp.bfloat16),
            jax.ShapeDtypeStruct((t, MLA_WIDTH), jnp.bfloat16),
        ],
        compiler_params=_params("parallel"),
        name="mla_pre",
    )(x2d, g, wm, qg, kvg, wq, wk, wv, cos_m, sin_m)


def _dil_proj_kernel(x_ref, g_ref, w_ref, cos_ref, sin_ref, *refs):
    n = len(DIL_PATTERNS)
    out_refs, planes = refs[:n], refs[n:]
    dils = [d for _, d in DIL_PATTERNS]
    tm = x_ref.shape[0]
    h = _rms(x_ref[...], g_ref[...]).astype(jnp.bfloat16)
    cos, sin = cos_ref[...], sin_ref[...]
    for which in range(3):
        y = _bf16_dot(h, w_ref[:, which * DIL_WIDTH:(which + 1) * DIL_WIDTH])
        for hd in range(DIL_HEADS):
            lanes = slice(hd * DIL_HD, (hd + 1) * DIL_HD)
            yh = y[:, lanes]
            if which < 2:
                yh = _rope128(yh, cos, sin)
            out_refs[0][which, 0, 0, :, lanes] = yh.astype(jnp.bfloat16)
            planes[0][hd] = yh
            for lvl in range(1, n):
                d_prev, d = dils[lvl - 1], dils[lvl]
                ratio, rows_prev, rows = d // d_prev, tm // d_prev, tm // d
                for r_prev in range(d_prev):
                    for sub in range(ratio):
                        c = planes[lvl - 1][hd, pl.ds(r_prev * rows_prev + sub, rows, stride=ratio), :]
                        cls = d_prev * sub + r_prev
                        out_refs[lvl][which, 0, cls, :, lanes] = c.astype(jnp.bfloat16)
                        if lvl + 1 < n:
                            planes[lvl][hd, cls * rows:(cls + 1) * rows, :] = c


def _dil_proj(x2d, batch, seq, g, w_dil, cos_d, sin_d, tm=512):
    t = x2d.shape[0]
    tiles = seq // tm
    dils = [d for _, d in DIL_PATTERNS]
    assert dils[0] == 1 and all(b % a == 0 for a, b in zip(dils, dils[1:]))
    const = lambda i: (0, 0)
    pos = lambda i: (i % tiles, 0)
    return pl.pallas_call(
        _dil_proj_kernel,
        grid=(t // tm,),
        in_specs=[
            pl.BlockSpec((tm, D_MODEL), lambda i: (i, 0)),
            pl.BlockSpec((1, D_MODEL), const),
            _resident(w_dil.shape),
            pl.BlockSpec((tm, LANES), pos),
            pl.BlockSpec((tm, LANES), pos),
        ],
        out_specs=[
            pl.BlockSpec((3, 1, d, tm // d, DIL_WIDTH), lambda i: (0, i // tiles, 0, i % tiles, 0))
            for d in dils
        ],
        out_shape=[jax.ShapeDtypeStruct((3, batch, d, seq // d, DIL_WIDTH), jnp.bfloat16) for d in dils],
        scratch_shapes=[pltpu.VMEM((DIL_HEADS, tm, DIL_HD), jnp.float32) for _ in dils[:-1]],
        compiler_params=_params("parallel"),
        name="dil_proj",
    )(x2d, g, w_dil, cos_d, sin_d)


def _mla_attn_kernel(q_ref, k_ref, v_ref, o_ref, vext_scr, *, chunk):
    exp2_scale = math.log2(math.e) / math.sqrt(NOPE_DIM + ROPE_DIM)

    heads = vext_scr.shape[0]

    @pl.when(pl.program_id(2) == 0)
    def _():
        for hd in range(heads):
            vext_scr[hd, :, :V_DIM] = v_ref[0, :, hd * V_DIM:(hd + 1) * V_DIM]
            vext_scr[hd, :, V_DIM:] = jnp.ones((vext_scr.shape[1], V_DIM), jnp.bfloat16)

    for hd in range(heads):
        q = q_ref[0, :, hd * MLA_QK:(hd + 1) * MLA_QK]
        m = acc = None
        for c in range(k_ref.shape[1] // chunk):
            rows = slice(c * chunk, (c + 1) * chunk)
            s = lax.dot_general(q, k_ref[0, rows, hd * MLA_QK:(hd + 1) * MLA_QK], _NT,
                                preferred_element_type=jnp.float32)
            m_c = jnp.max(s, axis=-1, keepdims=True)
            m_new = m_c if m is None else jnp.maximum(m, m_c)
            p = jnp.exp2((s - m_new) * exp2_scale).astype(jnp.bfloat16)
            pv = _bf16_dot(p, vext_scr[hd, rows, :])
            acc = pv if m is None else acc * jnp.exp2((m - m_new) * exp2_scale) + pv
            m = m_new
        o = acc[:, :V_DIM] / acc[:, V_DIM:V_DIM + 1]
        o_ref[0, :, hd * V_DIM:(hd + 1) * V_DIM] = o.astype(o_ref.dtype)


def _mla_attn(q, k, v, tq=1024, chunk=512, heads=2):
    b, s, _ = q.shape
    return pl.pallas_call(
        functools.partial(_mla_attn_kernel, chunk=chunk),
        grid=(b, MLA_HEADS // heads, s // tq),
        in_specs=[
            pl.BlockSpec((1, tq, heads * MLA_QK), lambda bi, h, i: (bi, i, h)),
            pl.BlockSpec((1, s, heads * MLA_QK), lambda bi, h, i: (bi, 0, h)),
            pl.BlockSpec((1, s, heads * V_DIM), lambda bi, h, i: (bi, 0, h)),
        ],
        out_specs=pl.BlockSpec((1, tq, heads * V_DIM), lambda bi, h, i: (bi, i, h)),
        out_shape=jax.ShapeDtypeStruct((b, s, MLA_WIDTH), jnp.bfloat16),
        scratch_shapes=[pltpu.VMEM((heads, s, 2 * V_DIM), jnp.bfloat16)],
        compiler_params=_params("parallel", "parallel", "arbitrary"),
        name="mla_attn",
    )(q, k, v)


def _dil_attn_kernel(*refs, block, sub, seq):
    n = len(DIL_PATTERNS)
    in_refs, o_ref, o_nat, l_nat = refs[:7 * n], refs[7 * n], refs[7 * n + 1], refs[7 * n + 2]
    i = pl.program_id(1)
    heads = o_nat.shape[1]
    inv_sqrt = 1.0 / math.sqrt(DIL_HD)
    for p, (window, dilation) in enumerate(DIL_PATTERNS):
        q_ref, kp_ref, kc_ref, kn_ref, vp_ref, vc_ref, vn_ref = in_refs[7 * p:7 * p + 7]
        half = window // (2 * dilation)
        rows_cls = block // dilation
        sub_p = min(sub, rows_cls)
        nk = sub_p + 2 * half
        rel = (lax.broadcasted_iota(jnp.int32, (sub_p, nk), 1) - half
               - lax.broadcasted_iota(jnp.int32, (sub_p, nk), 0))
        in_band = jnp.where(rel <= half, jnp.where(rel >= -half, 1, 0), 0)
        ones = jnp.ones((nk, DIL_HD), jnp.bfloat16)
        biases = []
        for st in range(rows_cls // sub_p):
            k_pos = i * rows_cls + st * sub_p - half + lax.broadcasted_iota(jnp.int32, (sub_p, nk), 1)
            valid = jnp.where(k_pos >= 0, jnp.where(k_pos < seq // dilation, in_band, 0), 0)
            biases.append(jnp.where(valid > 0, 0.0, MASK_VALUE).astype(jnp.float32))
        for cl in range(dilation):
            k = jnp.concatenate([kp_ref[0, 0, cl], kc_ref[0, 0, cl], kn_ref[0, 0, cl]], axis=0)
            v = jnp.concatenate([vp_ref[0, 0, cl], vc_ref[0, 0, cl], vn_ref[0, 0, cl]], axis=0)
            for st in range(rows_cls // sub_p):
                q = q_ref[0, 0, cl, st * sub_p:(st + 1) * sub_p, :]
                if dilation > 1:
                    tok = pl.ds(cl + dilation * st * sub_p, sub_p, stride=dilation)
                else:
                    tok = slice(st * sub_p, (st + 1) * sub_p)
                for hd in range(heads):
                    lanes = slice(hd * DIL_HD, (hd + 1) * DIL_HD)
                    kh = k[st * sub_p:st * sub_p + nk, lanes]
                    vh = v[st * sub_p:st * sub_p + nk, lanes]
                    s = lax.dot_general(q[:, lanes], kh, _NT, preferred_element_type=jnp.float32) + biases[st]
                    m = jnp.max(s, axis=-1, keepdims=True)
                    e = jnp.exp2((s - m) * (inv_sqrt * math.log2(math.e))).astype(jnp.bfloat16)
                    pv = _bf16_dot(e, jnp.concatenate([vh, ones], axis=1))
                    den = pv[:, DIL_HD:]
                    o_nat[p, hd, tok, :] = pv[:, :DIL_HD] / den
                    l_nat[p, hd, tok, :] = m * (inv_sqrt * math.log2(math.e)) + jnp.log2(den)

    for hd in range(heads):
        for c in range(block // ROW_CHUNK):
            rows = slice(c * ROW_CHUNK, (c + 1) * ROW_CHUNK)
            lses = [l_nat[p, hd, rows, :] for p in range(n)]
            mx = functools.reduce(jnp.maximum, lses)
            es = [jnp.exp2(l - mx) for l in lses]
            num = functools.reduce(jnp.add, [e * o_nat[p, hd, rows, :] for p, e in enumerate(es)])
            o_ref[0, rows, hd * DIL_HD:(hd + 1) * DIL_HD] = num / functools.reduce(jnp.add, es)


def _dil_attn(qkv_by_pattern, block=2048, sub=128, heads=2):
    _, batch, _, seq, _ = qkv_by_pattern[0].shape
    block = min(block, seq)
    width = heads * DIL_HD
    args, in_specs = [], []
    for qkv, (window, dilation) in zip(qkv_by_pattern, DIL_PATTERNS):
        half = window // (2 * dilation)
        rows_cls = block // dilation
        per_half = rows_cls // half
        last = seq // dilation // half - 1

        def spec(which, rows, row_block, dilation=dilation):
            return pl.BlockSpec((1, 1, dilation, rows, width),
                                lambda b, i, g: (which, b, 0, row_block(i), g))

        cur = lambda i: i
        prev = lambda i, per_half=per_half: jnp.maximum(i * per_half - 1, 0)
        nxt = lambda i, per_half=per_half, last=last: jnp.minimum((i + 1) * per_half, last)
        in_specs += [
            spec(0, rows_cls, cur),
            spec(1, half, prev), spec(1, rows_cls, cur), spec(1, half, nxt),
            spec(2, half, prev), spec(2, rows_cls, cur), spec(2, half, nxt),
        ]
        args += [qkv] * 7
    n = len(DIL_PATTERNS)
    return pl.pallas_call(
        functools.partial(_dil_attn_kernel, block=block, sub=sub, seq=seq),
        grid=(batch, seq // block, DIL_HEADS // heads),
        in_specs=in_specs,
        out_specs=pl.BlockSpec((1, block, width), lambda b, i, g: (b, i, g)),
        out_shape=jax.ShapeDtypeStruct((batch, seq, DIL_WIDTH), jnp.float32),
        scratch_shapes=[
            pltpu.VMEM((n, heads, block, DIL_HD), jnp.float32),
            pltpu.VMEM((n, heads, block, DIL_HD), jnp.float32),
        ],
        compiler_params=_params("parallel", "parallel", "parallel"),
        name="dil_attn",
    )(*args)


def _mix_out_kernel(x_ref, oa_ref, ob_ref, ga_ref, gb_ref, wo_ref, y_ref):
    for c in range(x_ref.shape[0] // ROW_CHUNK):
        rows = slice(c * ROW_CHUNK, (c + 1) * ROW_CHUNK)
        mix_a = _rms(oa_ref[rows, :].astype(jnp.float32), ga_ref[...]).astype(jnp.bfloat16)
        mix_b = _rms(ob_ref[rows, :], gb_ref[...]).astype(jnp.bfloat16)
        y = _bf16_dot(mix_a, wo_ref[:MLA_WIDTH, :]) + _bf16_dot(mix_b, wo_ref[MLA_WIDTH:, :])
        y_ref[rows, :] = x_ref[rows, :] + y


def _mix_out(x2d, o_a, o_b, ga, gb, wo, tm=512):
    t = x2d.shape[0]
    row = lambda i: (i, 0)
    const = lambda i: (0, 0)
    return pl.pallas_call(
        _mix_out_kernel,
        grid=(t // tm,),
        in_specs=[
            pl.BlockSpec((tm, D_MODEL), row),
            pl.BlockSpec((tm, MLA_WIDTH), row),
            pl.BlockSpec((tm, DIL_WIDTH), row),
            pl.BlockSpec((1, MLA_WIDTH), const),
            pl.BlockSpec((1, DIL_WIDTH), const),
            _resident(wo.shape),
        ],
        out_specs=pl.BlockSpec((tm, D_MODEL), row),
        out_shape=jax.ShapeDtypeStruct((t, D_MODEL), jnp.float32),
        compiler_params=_params("parallel"),
        name="mix_out",
    )(x2d, o_a, o_b, ga, gb, wo)


def _mlp_kernel(x_ref, g_ref, wu_ref, wd_ref, gf_ref, y_ref, h_scr, acc_scr):
    k = pl.program_id(1)

    @pl.when(k == 0)
    def _():
        x = x_ref[...]
        h_scr[...] = _rms(x, g_ref[...]).astype(jnp.bfloat16)
        acc_scr[...] = x

    u = _bf16_dot(h_scr[...], wu_ref[...])
    a = jnp.square(jnp.maximum(u, 0.0)).astype(jnp.bfloat16)
    acc_scr[...] += _bf16_dot(a, wd_ref[...])

    @pl.when(k == pl.num_programs(1) - 1)
    def _():
        y_ref[...] = _rms(acc_scr[...], gf_ref[...])


def _mlp(x2d, g, wu, wd, gf, tm=512, tf=1024):
    t = x2d.shape[0]
    return pl.pallas_call(
        _mlp_kernel,
        grid=(t // tm, D_FF // tf),
        in_specs=[
            pl.BlockSpec((tm, D_MODEL), lambda i, k: (i, 0)),
            pl.BlockSpec((1, D_MODEL), lambda i, k: (0, 0)),
            pl.BlockSpec((D_MODEL, tf), lambda i, k: (0, k)),
            pl.BlockSpec((tf, D_MODEL), lambda i, k: (k, 0)),
            pl.BlockSpec((1, D_MODEL), lambda i, k: (0, 0)),
        ],
        out_specs=pl.BlockSpec((tm, D_MODEL), lambda i, k: (i, 0)),
        out_shape=jax.ShapeDtypeStruct((t, D_MODEL), jnp.float32),
        scratch_shapes=[
            pltpu.VMEM((tm, D_MODEL), jnp.bfloat16),
            pltpu.VMEM((tm, D_MODEL), jnp.float32),
        ],
        compiler_params=_params("parallel", "arbitrary"),
        name="mlp",
    )(x2d, g, wu, wd, gf)


def _rope_tables(seq):
    pos = np.arange(seq, dtype=np.float64)

    def cos_sin(dim):
        inv = 1.0 / (ROPE_THETA ** (np.arange(0, dim, 2, dtype=np.float64) / dim))
        ang = pos[:, None] * inv[None, :]
        return np.cos(ang), np.sin(ang)

    c, s = cos_sin(DIL_HD)
    cos_d = np.concatenate([c, c], axis=-1)
    sin_d = np.concatenate([-s, s], axis=-1)
    c, s = cos_sin(ROPE_DIM)
    z = np.zeros_like(c)
    cos_m = np.concatenate([c, z, c, z], axis=-1)
    sin_m = np.concatenate([-s, z, s, z], axis=-1)
    return tuple(jnp.asarray(t, dtype=jnp.float32) for t in (cos_d, sin_d, cos_m, sin_m))


def _spread_rope_cols(w):
    z = jnp.zeros(w.shape[:-1] + (ROPE_HALF,), w.dtype)
    return jnp.concatenate([w[..., :ROPE_HALF], z, w[..., ROPE_HALF:], z], axis=-1)


def _prepare_weights(w_in, w_q_b, w_kv_b, w_o, w_up, w_down):
    bf = jnp.bfloat16
    n_lora = Q_LORA + KV_LORA
    wm = jnp.concatenate([w_in[:, :n_lora], _spread_rope_cols(w_in[:, n_lora:n_lora + ROPE_DIM])],
                         axis=-1).astype(bf)
    w_dil = w_in[:, n_lora + ROPE_DIM:].astype(bf)
    wq = w_q_b.reshape(Q_LORA, MLA_HEADS, NOPE_DIM + ROPE_DIM)
    wq = jnp.concatenate([wq[..., :NOPE_DIM], _spread_rope_cols(wq[..., NOPE_DIM:])], axis=-1)
    wq = wq.reshape(Q_LORA, MLA_HEADS * MLA_QK).astype(bf)
    wkv = w_kv_b.reshape(KV_LORA, MLA_HEADS, NOPE_DIM + V_DIM)
    wk = wkv[..., :NOPE_DIM].reshape(KV_LORA, MLA_HEADS * NOPE_DIM).astype(bf)
    wv = wkv[..., NOPE_DIM:].reshape(KV_LORA, MLA_WIDTH).astype(bf)
    return wm, w_dil, wq, wk, wv, w_o.astype(bf), w_up.astype(bf), w_down.astype(bf)


def _trunk(x, gains, weights):
    attn_g, qa_g, kva_g, mla_g, dil_g, mlp_g, final_g = gains
    wm, w_dil, wq, wk, wv, wo, wu, wd = weights
    batch, seq, _ = x.shape
    x2d = x.reshape(batch * seq, D_MODEL)
    cos_d, sin_d, cos_m, sin_m = _rope_tables(seq)

    q, k, v = _mla_pre(x2d, seq, attn_g, wm, qa_g, kva_g, wq, wk, wv, cos_m, sin_m)
    qkv_d = _dil_proj(x2d, batch, seq, attn_g, w_dil, cos_d, sin_d)
    o_a = _mla_attn(q.reshape(batch, seq, -1), k.reshape(batch, seq, -1), v.reshape(batch, seq, -1))
    o_b = _dil_attn(qkv_d)
    x1 = _mix_out(x2d, o_a.reshape(batch * seq, MLA_WIDTH), o_b.reshape(batch * seq, DIL_WIDTH),
                  mla_g, dil_g, wo)
    y = _mlp(x1, mlp_g, wu, wd, final_g)
    return y.reshape(batch, seq, D_MODEL)


def kernel(x_prompt, x_sample, attn_norm_g, w_in, q_a_norm_g, w_q_b, kv_a_norm_g, w_kv_b,
           mla_out_norm_g, dil_out_norm_g, w_o, mlp_norm_g, w_up, w_down, final_norm_g):
    assert w_in.shape[0] == 1, "single-layer block"
    weights = _prepare_weights(w_in[0], w_q_b[0], w_kv_b[0], w_o[0], w_up[0], w_down[0])
    gains = (attn_norm_g[0][None], q_a_norm_g[0][None], kv_a_norm_g[0][None],
             mla_out_norm_g[0][None], dil_out_norm_g[0][None], mlp_norm_g[0][None],
             final_norm_g[None])
    return (_trunk(x_prompt, gains, weights), _trunk(x_sample, gains, weights))
```

```python
import functools
import math

import numpy as np
import jax
import jax.numpy as jnp
from jax import lax
from jax.experimental import pallas as pl
from jax.experimental.pallas import tpu as pltpu

D_MODEL = 2048
EPS = 1e-6
ROPE_THETA = 10000.0
MLA_HEADS = 8
Q_LORA = 512
KV_LORA = 512
NOPE_DIM = 128
ROPE_DIM = 64
V_DIM = 128
MLA_WIDTH = MLA_HEADS * V_DIM
DIL_HEADS = 8
DIL_HD = 128
DIL_WIDTH = DIL_HEADS * DIL_HD
DIL_PATTERNS = ((128, 1), (512, 4), (2048, 16))
D_FF = 4 * D_MODEL

LANES = 128
VMEM_LIMIT_BYTES = 56 * 1024 * 1024
MXU_DIM = 256
ROW_CHUNK = 128

MLA_QK = NOPE_DIM + LANES
ROPE_HALF = ROPE_DIM // 2
MASK_VALUE = -1e30

_NT = (((1,), (1,)), ((), ()))


def _rms(x, g):
    return x * lax.rsqrt(jnp.mean(x * x, axis=-1, keepdims=True) + EPS) * g


def _rope128(x, cos, sin):
    return x * cos + pltpu.roll(x, LANES // 2, 1) * sin


def _bf16_dot(a, b):
    return jnp.dot(a, b, preferred_element_type=jnp.float32)


def _params(*sem):
    return pltpu.CompilerParams(dimension_semantics=sem, vmem_limit_bytes=VMEM_LIMIT_BYTES)


def _mla_pre_kernel(x_ref, g_ref, wm_ref, qg_ref, kvg_ref, wq_ref, wk_ref, wv_ref,
                    cos_ref, sin_ref, q_out, k_out, v_out):
    h = _rms(x_ref[...], g_ref[...]).astype(jnp.bfloat16)
    a = _bf16_dot(h, wm_ref[...])
    qn = _rms(a[:, :Q_LORA], qg_ref[...]).astype(jnp.bfloat16)
    kvn = _rms(a[:, Q_LORA:Q_LORA + KV_LORA], kvg_ref[...]).astype(jnp.bfloat16)
    cos, sin = cos_ref[...], sin_ref[...]
    k_rope = _rope128(a[:, Q_LORA + KV_LORA:], cos, sin).astype(jnp.bfloat16)
    q = _bf16_dot(qn, wq_ref[...])
    k_nope = _bf16_dot(kvn, wk_ref[...])
    for hd in range(MLA_HEADS):
        lo = hd * MLA_QK
        q_out[:, lo:lo + NOPE_DIM] = q[:, lo:lo + NOPE_DIM].astype(jnp.bfloat16)
        q_out[:, lo + NOPE_DIM:lo + MLA_QK] = _rope128(
            q[:, lo + NOPE_DIM:lo + MLA_QK], cos, sin).astype(jnp.bfloat16)
        k_out[:, lo:lo + NOPE_DIM] = k_nope[:, hd * NOPE_DIM:(hd + 1) * NOPE_DIM].astype(jnp.bfloat16)
        k_out[:, lo + NOPE_DIM:lo + MLA_QK] = k_rope
    v_out[...] = _bf16_dot(kvn, wv_ref[...]).astype(jnp.bfloat16)


def _resident(shape):
    return pl.BlockSpec(shape, lambda *_: (0,) * len(shape), pipeline_mode=pl.Buffered(1))


def _mla_pre(x2d, seq, g, wm, qg, kvg, wq, wk, wv, cos_m, sin_m, tm=512):
    t = x2d.shape[0]
    pos_blocks = seq // tm
    const = lambda i: (0, 0)
    row = lambda i: (i, 0)
    pos = lambda i: (i % pos_blocks, 0)
    return pl.pallas_call(
        _mla_pre_kernel,
        grid=(t // tm,),
        in_specs=[
            pl.BlockSpec((tm, D_MODEL), row),
            pl.BlockSpec((1, D_MODEL), const),
            _resident(wm.shape),
            pl.BlockSpec((1, Q_LORA), const),
            pl.BlockSpec((1, KV_LORA), const),
            _resident(wq.shape),
            _resident(wk.shape),
            _resident(wv.shape),
            pl.BlockSpec((tm, LANES), pos),
            pl.BlockSpec((tm, LANES), pos),
        ],
        out_specs=[
            pl.BlockSpec((tm, MLA_HEADS * MLA_QK), row),
            pl.BlockSpec((tm, MLA_HEADS * MLA_QK), row),
            pl.BlockSpec((tm, MLA_WIDTH), row),
        ],
        out_shape=[
            jax.ShapeDtypeStruct((t, MLA_HEADS * MLA_QK), jnp.bfloat16),
            jax.ShapeDtypeStruct((t, MLA_HEADS * MLA_QK), jnp.bfloat16),
            jax.ShapeDtypeStruct((t, MLA_WIDTH), jnp.bfloat16),
        ],
        compiler_params=_params("parallel"),
        name="mla_pre",
    )(x2d, g, wm, qg, kvg, wq, wk, wv, cos_m, sin_m)


def _dil_proj_kernel(x_ref, g_ref, w_ref, cos_ref, sin_ref, *refs):
    n = len(DIL_PATTERNS)
    out_refs, planes = refs[:n], refs[n:]
    dils = [d for _, d in DIL_PATTERNS]
    tm = x_ref.shape[0]
    h = _rms(x_ref[...], g_ref[...]).astype(jnp.bfloat16)
    cos, sin = cos_ref[...], sin_ref[...]
    for which in range(3):
        y = _bf16_dot(h, w_ref[:, which * DIL_WIDTH:(which + 1) * DIL_WIDTH])
        for hd in range(DIL_HEADS):
            lanes = slice(hd * DIL_HD, (hd + 1) * DIL_HD)
            yh = y[:, lanes]
            if which < 2:
                yh = _rope128(yh, cos, sin)
            out_refs[0][which, 0, 0, :, lanes] = yh.astype(jnp.bfloat16)
            planes[0][hd] = yh
            for lvl in range(1, n):
                d_prev, d = dils[lvl - 1], dils[lvl]
                ratio, rows_prev, rows = d // d_prev, tm // d_prev, tm // d
                for r_prev in range(d_prev):
                    for sub in range(ratio):
                        c = planes[lvl - 1][hd, pl.ds(r_prev * rows_prev + sub, rows, stride=ratio), :]
                        cls = d_prev * sub + r_prev
                        out_refs[lvl][which, 0, cls, :, lanes] = c.astype(jnp.bfloat16)
                        if lvl + 1 < n:
                            planes[lvl][hd, cls * rows:(cls + 1) * rows, :] = c


def _dil_proj(x2d, batch, seq, g, w_dil, cos_d, sin_d, tm=512):
    t = x2d.shape[0]
    tiles = seq // tm
    dils = [d for _, d in DIL_PATTERNS]
    assert dils[0] == 1 and all(b % a == 0 for a, b in zip(dils, dils[1:]))
    const = lambda i: (0, 0)
    pos = lambda i: (i % tiles, 0)
    return pl.pallas_call(
        _dil_proj_kernel,
        grid=(t // tm,),
        in_specs=[
            pl.BlockSpec((tm, D_MODEL), lambda i: (i, 0)),
            pl.BlockSpec((1, D_MODEL), const),
            _resident(w_dil.shape),
            pl.BlockSpec((tm, LANES), pos),
            pl.BlockSpec((tm, LANES), pos),
        ],
        out_specs=[
            pl.BlockSpec((3, 1, d, tm // d, DIL_WIDTH), lambda i: (0, i // tiles, 0, i % tiles, 0))
            for d in dils
        ],
        out_shape=[jax.ShapeDtypeStruct((3, batch, d, seq // d, DIL_WIDTH), jnp.bfloat16) for d in dils],
        scratch_shapes=[pltpu.VMEM((DIL_HEADS, tm, DIL_HD), jnp.float32) for _ in dils[:-1]],
        compiler_params=_params("parallel"),
        name="dil_proj",
    )(x2d, g, w_dil, cos_d, sin_d)


def _mla_attn_kernel(q_ref, k_ref, v_ref, o_ref, vext_scr, *, chunk):
    exp2_scale = math.log2(math.e) / math.sqrt(NOPE_DIM + ROPE_DIM)

    heads = vext_scr.shape[0]

    @pl.when(pl.program_id(2) == 0)
    def _():
        for hd in range(heads):
            vext_scr[hd, :, :V_DIM] = v_ref[0, :, hd * V_DIM:(hd + 1) * V_DIM]
            vext_scr[hd, :, V_DIM:] = jnp.ones((vext_scr.shape[1], V_DIM), jnp.bfloat16)

    for hd in range(heads):
        q = q_ref[0, :, hd * MLA_QK:(hd + 1) * MLA_QK]
        m = acc = None
        for c in range(k_ref.shape[1] // chunk):
            rows = slice(c * chunk, (c + 1) * chunk)
            s = lax.dot_general(q, k_ref[0, rows, hd * MLA_QK:(hd + 1) * MLA_QK], _NT,
                                preferred_element_type=jnp.float32)
            m_c = jnp.max(s, axis=-1, keepdims=True)
            m_new = m_c if m is None else jnp.maximum(m, m_c)
            p = jnp.exp2((s - m_new).astype(jnp.bfloat16) * jnp.bfloat16(exp2_scale))
            pv = _bf16_dot(p, vext_scr[hd, rows, :])
            acc = pv if m is None else acc * jnp.exp2((m - m_new) * exp2_scale) + pv
            m = m_new
        o = acc[:, :V_DIM] / acc[:, V_DIM:V_DIM + 1]
        o_ref[0, :, hd * V_DIM:(hd + 1) * V_DIM] = o.astype(o_ref.dtype)


def _mla_attn(q, k, v, tq=1024, chunk=512, heads=2):
    b, s, _ = q.shape
    return pl.pallas_call(
        functools.partial(_mla_attn_kernel, chunk=chunk),
        grid=(b, MLA_HEADS // heads, s // tq),
        in_specs=[
            pl.BlockSpec((1, tq, heads * MLA_QK), lambda bi, h, i: (bi, i, h)),
            pl.BlockSpec((1, s, heads * MLA_QK), lambda bi, h, i: (bi, 0, h)),
            pl.BlockSpec((1, s, heads * V_DIM), lambda bi, h, i: (bi, 0, h)),
        ],
        out_specs=pl.BlockSpec((1, tq, heads * V_DIM), lambda bi, h, i: (bi, i, h)),
        out_shape=jax.ShapeDtypeStruct((b, s, MLA_WIDTH), jnp.bfloat16),
        scratch_shapes=[pltpu.VMEM((heads, s, 2 * V_DIM), jnp.bfloat16)],
        compiler_params=_params("parallel", "parallel", "arbitrary"),
        name="mla_attn",
    )(q, k, v)


def _dil_attn_kernel(*refs, block, sub, seq):
    n = len(DIL_PATTERNS)
    in_refs, o_ref, o_nat, l_nat = refs[:7 * n], refs[7 * n], refs[7 * n + 1], refs[7 * n + 2]
    i = pl.program_id(1)
    heads = o_nat.shape[1]
    inv_sqrt = 1.0 / math.sqrt(DIL_HD)
    for p, (window, dilation) in enumerate(DIL_PATTERNS):
        q_ref, kp_ref, kc_ref, kn_ref, vp_ref, vc_ref, vn_ref = in_refs[7 * p:7 * p + 7]
        half = window // (2 * dilation)
        rows_cls = block // dilation
        sub_p = min(sub, rows_cls)
        nk = sub_p + 2 * half
        rel = (lax.broadcasted_iota(jnp.int32, (sub_p, nk), 1) - half
               - lax.broadcasted_iota(jnp.int32, (sub_p, nk), 0))
        in_band = jnp.where(rel <= half, jnp.where(rel >= -half, 1, 0), 0)
        ones = jnp.ones((nk, DIL_HD), jnp.bfloat16)
        biases = []
        for st in range(rows_cls // sub_p):
            k_pos = i * rows_cls + st * sub_p - half + lax.broadcasted_iota(jnp.int32, (sub_p, nk), 1)
            valid = jnp.where(k_pos >= 0, jnp.where(k_pos < seq // dilation, in_band, 0), 0)
            biases.append(jnp.where(valid > 0, 0.0, MASK_VALUE).astype(jnp.float32))
        for cl in range(dilation):
            k = jnp.concatenate([kp_ref[0, 0, cl], kc_ref[0, 0, cl], kn_ref[0, 0, cl]], axis=0)
            v = jnp.concatenate([vp_ref[0, 0, cl], vc_ref[0, 0, cl], vn_ref[0, 0, cl]], axis=0)
            for st in range(rows_cls // sub_p):
                q = q_ref[0, 0, cl, st * sub_p:(st + 1) * sub_p, :]
                if dilation > 1:
                    tok = pl.ds(cl + dilation * st * sub_p, sub_p, stride=dilation)
                else:
                    tok = slice(st * sub_p, (st + 1) * sub_p)
                for hd in range(heads):
                    lanes = slice(hd * DIL_HD, (hd + 1) * DIL_HD)
                    kh = k[st * sub_p:st * sub_p + nk, lanes]
                    vh = v[st * sub_p:st * sub_p + nk, lanes]
                    s = lax.dot_general(q[:, lanes], kh, _NT, preferred_element_type=jnp.float32) + biases[st]
                    m = jnp.max(s, axis=-1, keepdims=True)
                    e = jnp.exp2((s - m).astype(jnp.bfloat16) * jnp.bfloat16(inv_sqrt * math.log2(math.e)))
                    pv = _bf16_dot(e, jnp.concatenate([vh, ones], axis=1))
                    den = pv[:, DIL_HD:]
                    o_nat[p, hd, tok, :] = pv[:, :DIL_HD] / den
                    l_nat[p, hd, tok, :] = m * (inv_sqrt * math.log2(math.e)) + jnp.log2(den)

    for hd in range(heads):
        for c in range(block // ROW_CHUNK):
            rows = slice(c * ROW_CHUNK, (c + 1) * ROW_CHUNK)
            lses = [l_nat[p, hd, rows, :] for p in range(n)]
            mx = functools.reduce(jnp.maximum, lses)
            es = [jnp.exp2(l - mx) for l in lses]
            num = functools.reduce(jnp.add, [e * o_nat[p, hd, rows, :] for p, e in enumerate(es)])
            o_ref[0, rows, hd * DIL_HD:(hd + 1) * DIL_HD] = num / functools.reduce(jnp.add, es)


def _dil_attn(qkv_by_pattern, block=2048, sub=128, heads=2):
    _, batch, _, seq, _ = qkv_by_pattern[0].shape
    block = min(block, seq)
    width = heads * DIL_HD
    args, in_specs = [], []
    for qkv, (window, dilation) in zip(qkv_by_pattern, DIL_PATTERNS):
        half = window // (2 * dilation)
        rows_cls = block // dilation
        per_half = rows_cls // half
        last = seq // dilation // half - 1

        def spec(which, rows, row_block, dilation=dilation):
            return pl.BlockSpec((1, 1, dilation, rows, width),
                                lambda b, i, g: (which, b, 0, row_block(i), g))

        cur = lambda i: i
        prev = lambda i, per_half=per_half: jnp.maximum(i * per_half - 1, 0)
        nxt = lambda i, per_half=per_half, last=last: jnp.minimum((i + 1) * per_half, last)
        in_specs += [
            spec(0, rows_cls, cur),
            spec(1, half, prev), spec(1, rows_cls, cur), spec(1, half, nxt),
            spec(2, half, prev), spec(2, rows_cls, cur), spec(2, half, nxt),
        ]
        args += [qkv] * 7
    n = len(DIL_PATTERNS)
    return pl.pallas_call(
        functools.partial(_dil_attn_kernel, block=block, sub=sub, seq=seq),
        grid=(batch, seq // block, DIL_HEADS // heads),
        in_specs=in_specs,
        out_specs=pl.BlockSpec((1, block, width), lambda b, i, g: (b, i, g)),
        out_shape=jax.ShapeDtypeStruct((batch, seq, DIL_WIDTH), jnp.float32),
        scratch_shapes=[
            pltpu.VMEM((n, heads, block, DIL_HD), jnp.float32),
            pltpu.VMEM((n, heads, block, DIL_HD), jnp.float32),
        ],
        compiler_params=_params("parallel", "parallel", "parallel"),
        name="dil_attn",
    )(*args)


def _mix_out_kernel(x_ref, oa_ref, ob_ref, ga_ref, gb_ref, wo_ref, y_ref):
    for c in range(x_ref.shape[0] // ROW_CHUNK):
        rows = slice(c * ROW_CHUNK, (c + 1) * ROW_CHUNK)
        mix_a = _rms(oa_ref[rows, :].astype(jnp.float32), ga_ref[...]).astype(jnp.bfloat16)
        mix_b = _rms(ob_ref[rows, :], gb_ref[...]).astype(jnp.bfloat16)
        y = _bf16_dot(mix_a, wo_ref[:MLA_WIDTH, :]) + _bf16_dot(mix_b, wo_ref[MLA_WIDTH:, :])
        y_ref[rows, :] = x_ref[rows, :] + y


def _mix_out(x2d, o_a, o_b, ga, gb, wo, tm=512):
    t = x2d.shape[0]
    row = lambda i: (i, 0)
    const = lambda i: (0, 0)
    return pl.pallas_call(
        _mix_out_kernel,
        grid=(t // tm,),
        in_specs=[
            pl.BlockSpec((tm, D_MODEL), row),
            pl.BlockSpec((tm, MLA_WIDTH), row),
            pl.BlockSpec((tm, DIL_WIDTH), row),
            pl.BlockSpec((1, MLA_WIDTH), const),
            pl.BlockSpec((1, DIL_WIDTH), const),
            _resident(wo.shape),
        ],
        out_specs=pl.BlockSpec((tm, D_MODEL), row),
        out_shape=jax.ShapeDtypeStruct((t, D_MODEL), jnp.float32),
        compiler_params=_params("parallel"),
        name="mix_out",
    )(x2d, o_a, o_b, ga, gb, wo)


def _mlp_kernel(x_ref, g_ref, wu_ref, wd_ref, gf_ref, y_ref, h_scr, acc_scr):
    k = pl.program_id(1)

    @pl.when(k == 0)
    def _():
        x = x_ref[...]
        h_scr[...] = _rms(x, g_ref[...]).astype(jnp.bfloat16)
        acc_scr[...] = x

    u = _bf16_dot(h_scr[...], wu_ref[...])
    a = jnp.square(jnp.maximum(u, 0.0)).astype(jnp.bfloat16)
    acc_scr[...] += _bf16_dot(a, wd_ref[...])

    @pl.when(k == pl.num_programs(1) - 1)
    def _():
        y_ref[...] = _rms(acc_scr[...], gf_ref[...])


def _mlp(x2d, g, wu, wd, gf, tm=512, tf=1024):
    t = x2d.shape[0]
    return pl.pallas_call(
        _mlp_kernel,
        grid=(t // tm, D_FF // tf),
        in_specs=[
            pl.BlockSpec((tm, D_MODEL), lambda i, k: (i, 0)),
            pl.BlockSpec((1, D_MODEL), lambda i, k: (0, 0)),
            pl.BlockSpec((D_MODEL, tf), lambda i, k: (0, k)),
            pl.BlockSpec((tf, D_MODEL), lambda i, k: (k, 0)),
            pl.BlockSpec((1, D_MODEL), lambda i, k: (0, 0)),
        ],
        out_specs=pl.BlockSpec((tm, D_MODEL), lambda i, k: (i, 0)),
        out_shape=jax.ShapeDtypeStruct((t, D_MODEL), jnp.float32),
        scratch_shapes=[
            pltpu.VMEM((tm, D_MODEL), jnp.bfloat16),
            pltpu.VMEM((tm, D_MODEL), jnp.float32),
        ],
        compiler_params=_params("parallel", "arbitrary"),
        name="mlp",
    )(x2d, g, wu, wd, gf)


def _rope_tables(seq):
    pos = np.arange(seq, dtype=np.float64)

    def cos_sin(dim):
        inv = 1.0 / (ROPE_THETA ** (np.arange(0, dim, 2, dtype=np.float64) / dim))
        ang = pos[:, None] * inv[None, :]
        return np.cos(ang), np.sin(ang)

    c, s = cos_sin(DIL_HD)
    cos_d = np.concatenate([c, c], axis=-1)
    sin_d = np.concatenate([-s, s], axis=-1)
    c, s = cos_sin(ROPE_DIM)
    z = np.zeros_like(c)
    cos_m = np.concatenate([c, z, c, z], axis=-1)
    sin_m = np.concatenate([-s, z, s, z], axis=-1)
    return tuple(jnp.asarray(t, dtype=jnp.float32) for t in (cos_d, sin_d, cos_m, sin_m))


def _spread_rope_cols(w):
    z = jnp.zeros(w.shape[:-1] + (ROPE_HALF,), w.dtype)
    return jnp.concatenate([w[..., :ROPE_HALF], z, w[..., ROPE_HALF:], z], axis=-1)


def _prepare_weights(w_in, w_q_b, w_kv_b, w_o, w_up, w_down):
    bf = jnp.bfloat16
    n_lora = Q_LORA + KV_LORA
    wm = jnp.concatenate([w_in[:, :n_lora], _spread_rope_cols(w_in[:, n_lora:n_lora + ROPE_DIM])],
                         axis=-1).astype(bf)
    w_dil = w_in[:, n_lora + ROPE_DIM:].astype(bf)
    wq = w_q_b.reshape(Q_LORA, MLA_HEADS, NOPE_DIM + ROPE_DIM)
    wq = jnp.concatenate([wq[..., :NOPE_DIM], _spread_rope_cols(wq[..., NOPE_DIM:])], axis=-1)
    wq = wq.reshape(Q_LORA, MLA_HEADS * MLA_QK).astype(bf)
    wkv = w_kv_b.reshape(KV_LORA, MLA_HEADS, NOPE_DIM + V_DIM)
    wk = wkv[..., :NOPE_DIM].reshape(KV_LORA, MLA_HEADS * NOPE_DIM).astype(bf)
    wv = wkv[..., NOPE_DIM:].reshape(KV_LORA, MLA_WIDTH).astype(bf)
    return wm, w_dil, wq, wk, wv, w_o.astype(bf), w_up.astype(bf), w_down.astype(bf)


def _trunk(x, gains, weights):
    attn_g, qa_g, kva_g, mla_g, dil_g, mlp_g, final_g = gains
    wm, w_dil, wq, wk, wv, wo, wu, wd = weights
    batch, seq, _ = x.shape
    x2d = x.reshape(batch * seq, D_MODEL)
    cos_d, sin_d, cos_m, sin_m = _rope_tables(seq)

    q, k, v = _mla_pre(x2d, seq, attn_g, wm, qa_g, kva_g, wq, wk, wv, cos_m, sin_m)
    qkv_d = _dil_proj(x2d, batch, seq, attn_g, w_dil, cos_d, sin_d)
    o_a = _mla_attn(q.reshape(batch, seq, -1), k.reshape(batch, seq, -1), v.reshape(batch, seq, -1))
    o_b = _dil_attn(qkv_d)
    x1 = _mix_out(x2d, o_a.reshape(batch * seq, MLA_WIDTH), o_b.reshape(batch * seq, DIL_WIDTH),
                  mla_g, dil_g, wo)
    y = _mlp(x1, mlp_g, wu, wd, final_g)
    return y.reshape(batch, seq, D_MODEL)


def kernel(x_prompt, x_sample, attn_norm_g, w_in, q_a_norm_g, w_q_b, kv_a_norm_g, w_kv_b,
           mla_out_norm_g, dil_out_norm_g, w_o, mlp_norm_g, w_up, w_down, final_norm_g):
    assert w_in.shape[0] == 1, "single-layer block"
    weights = _prepare_weights(w_in[0], w_q_b[0], w_kv_b[0], w_o[0], w_up[0], w_down[0])
    gains = (attn_norm_g[0][None], q_a_norm_g[0][None], kv_a_norm_g[0][None],
             mla_out_norm_g[0][None], dil_out_norm_g[0][None], mlp_norm_g[0][None],
             final_norm_g[None])
    return (_trunk(x_prompt, gains, weights), _trunk(x_sample, gains, weights))
```

```python
import functools
import math

import numpy as np
import jax
import jax.numpy as jnp
from jax import lax
from jax.experimental import pallas as pl
from jax.experimental.pallas import tpu as pltpu

D_MODEL = 2048
EPS = 1e-6
ROPE_THETA = 10000.0
MLA_HEADS = 8
Q_LORA = 512
KV_LORA = 512
NOPE_DIM = 128
ROPE_DIM = 64
V_DIM = 128
MLA_WIDTH = MLA_HEADS * V_DIM
DIL_HEADS = 8
DIL_HD = 128
DIL_WIDTH = DIL_HEADS * DIL_HD
DIL_PATTERNS = ((128, 1), (512, 4), (2048, 16))
D_FF = 4 * D_MODEL

LANES = 128
VMEM_LIMIT_BYTES = 56 * 1024 * 1024
MXU_DIM = 256
ROW_CHUNK = 128

MLA_QK = NOPE_DIM + LANES
ROPE_HALF = ROPE_DIM // 2
MASK_VALUE = -1e30

MLA_Q_SCALE = math.log2(math.e) / math.sqrt(NOPE_DIM + ROPE_DIM)
DIL_Q_SCALE = math.log2(math.e) / math.sqrt(DIL_HD)

_NT = (((1,), (1,)), ((), ()))


def _rms(x, g):
    return x * lax.rsqrt(jnp.mean(x * x, axis=-1, keepdims=True) + EPS) * g


def _rope128(x, cos, sin):
    return x * cos + pltpu.roll(x, LANES // 2, 1) * sin


def _bf16_dot(a, b):
    return jnp.dot(a, b, preferred_element_type=jnp.float32)


def _params(*sem):
    return pltpu.CompilerParams(dimension_semantics=sem, vmem_limit_bytes=VMEM_LIMIT_BYTES)


def _mla_pre_kernel(x_ref, g_ref, wm_ref, qg_ref, kvg_ref, wq_ref, wk_ref, wv_ref,
                    cos_ref, sin_ref, q_out, k_out, v_out):
    h = _rms(x_ref[...], g_ref[...]).astype(jnp.bfloat16)
    a = _bf16_dot(h, wm_ref[...])
    qn = _rms(a[:, :Q_LORA], qg_ref[...]).astype(jnp.bfloat16)
    kvn = _rms(a[:, Q_LORA:Q_LORA + KV_LORA], kvg_ref[...]).astype(jnp.bfloat16)
    cos, sin = cos_ref[...], sin_ref[...]
    k_rope = _rope128(a[:, Q_LORA + KV_LORA:], cos, sin).astype(jnp.bfloat16)
    q = _bf16_dot(qn, wq_ref[...]) * MLA_Q_SCALE
    k_nope = _bf16_dot(kvn, wk_ref[...])
    for hd in range(MLA_HEADS):
        lo = hd * MLA_QK
        q_out[:, lo:lo + NOPE_DIM] = q[:, lo:lo + NOPE_DIM].astype(jnp.bfloat16)
        q_out[:, lo + NOPE_DIM:lo + MLA_QK] = _rope128(
            q[:, lo + NOPE_DIM:lo + MLA_QK], cos, sin).astype(jnp.bfloat16)
        k_out[:, lo:lo + NOPE_DIM] = k_nope[:, hd * NOPE_DIM:(hd + 1) * NOPE_DIM].astype(jnp.bfloat16)
        k_out[:, lo + NOPE_DIM:lo + MLA_QK] = k_rope
    v_out[...] = _bf16_dot(kvn, wv_ref[...]).astype(jnp.bfloat16)


def _resident(shape):
    return pl.BlockSpec(shape, lambda *_: (0,) * len(shape), pipeline_mode=pl.Buffered(1))


def _mla_pre(x2d, seq, g, wm, qg, kvg, wq, wk, wv, cos_m, sin_m, tm=512):
    t = x2d.shape[0]
    pos_blocks = seq // tm
    const = lambda i: (0, 0)
    row = lambda i: (i, 0)
    pos = lambda i: (i % pos_blocks, 0)
    return pl.pallas_call(
        _mla_pre_kernel,
        grid=(t // tm,),
        in_specs=[
            pl.BlockSpec((tm, D_MODEL), row),
            pl.BlockSpec((1, D_MODEL), const),
            _resident(wm.shape),
            pl.BlockSpec((1, Q_LORA), const),
            pl.BlockSpec((1, KV_LORA), const),
            _resident(wq.shape),
            _resident(wk.shape),
            _resident(wv.shape),
            pl.BlockSpec((tm, LANES), pos),
            pl.BlockSpec((tm, LANES), pos),
        ],
        out_specs=[
            pl.BlockSpec((tm, MLA_HEADS * MLA_QK), row),
            pl.BlockSpec((tm, MLA_HEADS * MLA_QK), row),
            pl.BlockSpec((tm, MLA_WIDTH), row),
        ],
        out_shape=[
            jax.ShapeDtypeStruct((t, MLA_HEADS * MLA_QK), jnp.bfloat16),
            jax.ShapeDtypeStruct((t, MLA_HEADS * MLA_QK), jnp.bfloat16),
            jax.ShapeDtypeStruct((t, MLA_WIDTH), jnp.bfloat16),
        ],
        compiler_params=_params("parallel"),
        name="mla_pre",
    )(x2d, g, wm, qg, kvg, wq, wk, wv, cos_m, sin_m)


def _dil_proj_kernel(x_ref, g_ref, w_ref, cos_ref, sin_ref, *refs):
    n = len(DIL_PATTERNS)
    out_refs, planes = refs[:n], refs[n:]
    dils = [d for _, d in DIL_PATTERNS]
    tm = x_ref.shape[0]
    h = _rms(x_ref[...], g_ref[...]).astype(jnp.bfloat16)
    cos, sin = cos_ref[...], sin_ref[...]
    for which in range(3):
        y = _bf16_dot(h, w_ref[:, which * DIL_WIDTH:(which + 1) * DIL_WIDTH])
        for hd in range(DIL_HEADS):
            lanes = slice(hd * DIL_HD, (hd + 1) * DIL_HD)
            yh = y[:, lanes]
            if which < 2:
                yh = _rope128(yh, cos, sin)
            if which == 0:
                yh = yh * DIL_Q_SCALE
            out_refs[0][which, 0, 0, :, lanes] = yh.astype(jnp.bfloat16)
            planes[0][hd] = yh
            for lvl in range(1, n):
                d_prev, d = dils[lvl - 1], dils[lvl]
                ratio, rows_prev, rows = d // d_prev, tm // d_prev, tm // d
                for r_prev in range(d_prev):
                    for sub in range(ratio):
                        c = planes[lvl - 1][hd, pl.ds(r_prev * rows_prev + sub, rows, stride=ratio), :]
                        cls = d_prev * sub + r_prev
                        out_refs[lvl][which, 0, cls, :, lanes] = c.astype(jnp.bfloat16)
                        if lvl + 1 < n:
                            planes[lvl][hd, cls * rows:(cls + 1) * rows, :] = c


def _dil_proj(x2d, batch, seq, g, w_dil, cos_d, sin_d, tm=512):
    t = x2d.shape[0]
    tiles = seq // tm
    dils = [d for _, d in DIL_PATTERNS]
    assert dils[0] == 1 and all(b % a == 0 for a, b in zip(dils, dils[1:]))
    const = lambda i: (0, 0)
    pos = lambda i: (i % tiles, 0)
    return pl.pallas_call(
        _dil_proj_kernel,
        grid=(t // tm,),
        in_specs=[
            pl.BlockSpec((tm, D_MODEL), lambda i: (i, 0)),
            pl.BlockSpec((1, D_MODEL), const),
            _resident(w_dil.shape),
            pl.BlockSpec((tm, LANES), pos),
            pl.BlockSpec((tm, LANES), pos),
        ],
        out_specs=[
            pl.BlockSpec((3, 1, d, tm // d, DIL_WIDTH), lambda i: (0, i // tiles, 0, i % tiles, 0))
            for d in dils
        ],
        out_shape=[jax.ShapeDtypeStruct((3, batch, d, seq // d, DIL_WIDTH), jnp.bfloat16) for d in dils],
        scratch_shapes=[pltpu.VMEM((DIL_HEADS, tm, DIL_HD), jnp.float32) for _ in dils[:-1]],
        compiler_params=_params("parallel"),
        name="dil_proj",
    )(x2d, g, w_dil, cos_d, sin_d)


def _mla_attn_kernel(q_ref, k_ref, v_ref, o_ref, vext_scr, *, chunk):
    heads = vext_scr.shape[0]

    @pl.when(pl.program_id(2) == 0)
    def _():
        for hd in range(heads):
            vext_scr[hd, :, :V_DIM] = v_ref[0, :, hd * V_DIM:(hd + 1) * V_DIM]
            vext_scr[hd, :, V_DIM:] = jnp.ones((vext_scr.shape[1], V_DIM), jnp.bfloat16)

    for hd in range(heads):
        q = q_ref[0, :, hd * MLA_QK:(hd + 1) * MLA_QK]
        m = acc = None
        for c in range(k_ref.shape[1] // chunk):
            rows = slice(c * chunk, (c + 1) * chunk)
            s = lax.dot_general(q, k_ref[0, rows, hd * MLA_QK:(hd + 1) * MLA_QK], _NT,
                                preferred_element_type=jnp.float32)
            m_c = jnp.max(s, axis=-1, keepdims=True)
            m_new = m_c if m is None else jnp.maximum(m, m_c)
            p = jnp.exp2((s - m_new).astype(jnp.bfloat16))
            pv = _bf16_dot(p, vext_scr[hd, rows, :])
            acc = pv if m is None else acc * jnp.exp2(m - m_new) + pv
            m = m_new
        o = acc[:, :V_DIM] / acc[:, V_DIM:V_DIM + 1]
        o_ref[0, :, hd * V_DIM:(hd + 1) * V_DIM] = o.astype(o_ref.dtype)


def _mla_attn(q, k, v, tq=1024, chunk=512, heads=2):
    b, s, _ = q.shape
    return pl.pallas_call(
        functools.partial(_mla_attn_kernel, chunk=chunk),
        grid=(b, MLA_HEADS // heads, s // tq),
        in_specs=[
            pl.BlockSpec((1, tq, heads * MLA_QK), lambda bi, h, i: (bi, i, h)),
            pl.BlockSpec((1, s, heads * MLA_QK), lambda bi, h, i: (bi, 0, h)),
            pl.BlockSpec((1, s, heads * V_DIM), lambda bi, h, i: (bi, 0, h)),
        ],
        out_specs=pl.BlockSpec((1, tq, heads * V_DIM), lambda bi, h, i: (bi, i, h)),
        out_shape=jax.ShapeDtypeStruct((b, s, MLA_WIDTH), jnp.bfloat16),
        scratch_shapes=[pltpu.VMEM((heads, s, 2 * V_DIM), jnp.bfloat16)],
        compiler_params=_params("parallel", "parallel", "arbitrary"),
        name="mla_attn",
    )(q, k, v)


def _dil_attn_kernel(*refs, block, sub, seq):
    n = len(DIL_PATTERNS)
    in_refs, o_ref, o_nat, l_nat = refs[:7 * n], refs[7 * n], refs[7 * n + 1], refs[7 * n + 2]
    i = pl.program_id(1)
    heads = o_nat.shape[1]
    for p, (window, dilation) in enumerate(DIL_PATTERNS):
        q_ref, kp_ref, kc_ref, kn_ref, vp_ref, vc_ref, vn_ref = in_refs[7 * p:7 * p + 7]
        half = window // (2 * dilation)
        rows_cls = block // dilation
        sub_p = min(sub, rows_cls)
        nk = sub_p + 2 * half
        rel = (lax.broadcasted_iota(jnp.int32, (sub_p, nk), 1) - half
               - lax.broadcasted_iota(jnp.int32, (sub_p, nk), 0))
        in_band = jnp.where(rel <= half, jnp.where(rel >= -half, 1, 0), 0)
        ones = jnp.ones((nk, DIL_HD), jnp.bfloat16)
        biases = []
        for st in range(rows_cls // sub_p):
            k_pos = i * rows_cls + st * sub_p - half + lax.broadcasted_iota(jnp.int32, (sub_p, nk), 1)
            valid = jnp.where(k_pos >= 0, jnp.where(k_pos < seq // dilation, in_band, 0), 0)
            biases.append(jnp.where(valid > 0, 0.0, MASK_VALUE).astype(jnp.float32))
        for cl in range(dilation):
            k = jnp.concatenate([kp_ref[0, 0, cl], kc_ref[0, 0, cl], kn_ref[0, 0, cl]], axis=0)
            v = jnp.concatenate([vp_ref[0, 0, cl], vc_ref[0, 0, cl], vn_ref[0, 0, cl]], axis=0)
            for st in range(rows_cls // sub_p):
                q = q_ref[0, 0, cl, st * sub_p:(st + 1) * sub_p, :]
                if dilation > 1:
                    tok = pl.ds(cl + dilation * st * sub_p, sub_p, stride=dilation)
                else:
                    tok = slice(st * sub_p, (st + 1) * sub_p)
                for hd in range(heads):
                    lanes = slice(hd * DIL_HD, (hd + 1) * DIL_HD)
                    kh = k[st * sub_p:st * sub_p + nk, lanes]
                    vh = v[st * sub_p:st * sub_p + nk, lanes]
                    s = lax.dot_general(q[:, lanes], kh, _NT, preferred_element_type=jnp.float32) + biases[st]
                    m = jnp.max(s, axis=-1, keepdims=True)
                    e = jnp.exp2((s - m).astype(jnp.bfloat16))
                    pv = _bf16_dot(e, jnp.concatenate([vh, ones], axis=1))
                    den = pv[:, DIL_HD:]
                    o_nat[p, hd, tok, :] = pv[:, :DIL_HD] / den
                    l_nat[p, hd, tok, :] = m + jnp.log2(den)

    for hd in range(heads):
        for c in range(block // ROW_CHUNK):
            rows = slice(c * ROW_CHUNK, (c + 1) * ROW_CHUNK)
            lses = [l_nat[p, hd, rows, :] for p in range(n)]
            mx = functools.reduce(jnp.maximum, lses)
            es = [jnp.exp2(l - mx) for l in lses]
            num = functools.reduce(jnp.add, [e * o_nat[p, hd, rows, :] for p, e in enumerate(es)])
            o_ref[0, rows, hd * DIL_HD:(hd + 1) * DIL_HD] = num / functools.reduce(jnp.add, es)


def _dil_attn(qkv_by_pattern, block=2048, sub=128, heads=2):
    _, batch, _, seq, _ = qkv_by_pattern[0].shape
    block = min(block, seq)
    width = heads * DIL_HD
    args, in_specs = [], []
    for qkv, (window, dilation) in zip(qkv_by_pattern, DIL_PATTERNS):
        half = window // (2 * dilation)
        rows_cls = block // dilation
        per_half = rows_cls // half
        last = seq // dilation // half - 1

        def spec(which, rows, row_block, dilation=dilation):
            return pl.BlockSpec((1, 1, dilation, rows, width),
                                lambda b, i, g: (which, b, 0, row_block(i), g))

        cur = lambda i: i
        prev = lambda i, per_half=per_half: jnp.maximum(i * per_half - 1, 0)
        nxt = lambda i, per_half=per_half, last=last: jnp.minimum((i + 1) * per_half, last)
        in_specs += [
            spec(0, rows_cls, cur),
            spec(1, half, prev), spec(1, rows_cls, cur), spec(1, half, nxt),
            spec(2, half, prev), spec(2, rows_cls, cur), spec(2, half, nxt),
        ]
        args += [qkv] * 7
    n = len(DIL_PATTERNS)
    return pl.pallas_call(
        functools.partial(_dil_attn_kernel, block=block, sub=sub, seq=seq),
        grid=(batch, seq // block, DIL_HEADS // heads),
        in_specs=in_specs,
        out_specs=pl.BlockSpec((1, block, width), lambda b, i, g: (b, i, g)),
        out_shape=jax.ShapeDtypeStruct((batch, seq, DIL_WIDTH), jnp.float32),
        scratch_shapes=[
            pltpu.VMEM((n, heads, block, DIL_HD), jnp.float32),
            pltpu.VMEM((n, heads, block, DIL_HD), jnp.float32),
        ],
        compiler_params=_params("parallel", "parallel", "parallel"),
        name="dil_attn",
    )(*args)


def _mix_out_kernel(x_ref, oa_ref, ob_ref, ga_ref, gb_ref, wo_ref, y_ref):
    for c in range(x_ref.shape[0] // ROW_CHUNK):
        rows = slice(c * ROW_CHUNK, (c + 1) * ROW_CHUNK)
        mix_a = _rms(oa_ref[rows, :].astype(jnp.float32), ga_ref[...]).astype(jnp.bfloat16)
        mix_b = _rms(ob_ref[rows, :], gb_ref[...]).astype(jnp.bfloat16)
        y = _bf16_dot(mix_a, wo_ref[:MLA_WIDTH, :]) + _bf16_dot(mix_b, wo_ref[MLA_WIDTH:, :])
        y_ref[rows, :] = x_ref[rows, :] + y


def _mix_out(x2d, o_a, o_b, ga, gb, wo, tm=512):
    t = x2d.shape[0]
    row = lambda i: (i, 0)
    const = lambda i: (0, 0)
    return pl.pallas_call(
        _mix_out_kernel,
        grid=(t // tm,),
        in_specs=[
            pl.BlockSpec((tm, D_MODEL), row),
            pl.BlockSpec((tm, MLA_WIDTH), row),
            pl.BlockSpec((tm, DIL_WIDTH), row),
            pl.BlockSpec((1, MLA_WIDTH), const),
            pl.BlockSpec((1, DIL_WIDTH), const),
            _resident(wo.shape),
        ],
        out_specs=pl.BlockSpec((tm, D_MODEL), row),
        out_shape=jax.ShapeDtypeStruct((t, D_MODEL), jnp.float32),
        compiler_params=_params("parallel"),
        name="mix_out",
    )(x2d, o_a, o_b, ga, gb, wo)


def _mlp_kernel(x_ref, g_ref, wu_ref, wd_ref, gf_ref, y_ref, h_scr, acc_scr):
    k = pl.program_id(1)

    @pl.when(k == 0)
    def _():
        x = x_ref[...]
        h_scr[...] = _rms(x, g_ref[...]).astype(jnp.bfloat16)
        acc_scr[...] = x

    u = _bf16_dot(h_scr[...], wu_ref[...])
    a = jnp.square(jnp.maximum(u, 0.0)).astype(jnp.bfloat16)
    acc_scr[...] += _bf16_dot(a, wd_ref[...])

    @pl.when(k == pl.num_programs(1) - 1)
    def _():
        y_ref[...] = _rms(acc_scr[...], gf_ref[...])


def _mlp(x2d, g, wu, wd, gf, tm=512, tf=1024):
    t = x2d.shape[0]
    return pl.pallas_call(
        _mlp_kernel,
        grid=(t // tm, D_FF // tf),
        in_specs=[
            pl.BlockSpec((tm, D_MODEL), lambda i, k: (i, 0)),
            pl.BlockSpec((1, D_MODEL), lambda i, k: (0, 0)),
            pl.BlockSpec((D_MODEL, tf), lambda i, k: (0, k)),
            pl.BlockSpec((tf, D_MODEL), lambda i, k: (k, 0)),
            pl.BlockSpec((1, D_MODEL), lambda i, k: (0, 0)),
        ],
        out_specs=pl.BlockSpec((tm, D_MODEL), lambda i, k: (i, 0)),
        out_shape=jax.ShapeDtypeStruct((t, D_MODEL), jnp.float32),
        scratch_shapes=[
            pltpu.VMEM((tm, D_MODEL), jnp.bfloat16),
            pltpu.VMEM((tm, D_MODEL), jnp.float32),
        ],
        compiler_params=_params("parallel", "arbitrary"),
        name="mlp",
    )(x2d, g, wu, wd, gf)


def _rope_tables(seq):
    pos = np.arange(seq, dtype=np.float64)

    def cos_sin(dim):
        inv = 1.0 / (ROPE_THETA ** (np.arange(0, dim, 2, dtype=np.float64) / dim))
        ang = pos[:, None] * inv[None, :]
        return np.cos(ang), np.sin(ang)

    c, s = cos_sin(DIL_HD)
    cos_d = np.concatenate([c, c], axis=-1)
    sin_d = np.concatenate([-s, s], axis=-1)
    c, s = cos_sin(ROPE_DIM)
    z = np.zeros_like(c)
    cos_m = np.concatenate([c, z, c, z], axis=-1)
    sin_m = np.concatenate([-s, z, s, z], axis=-1)
    return tuple(jnp.asarray(t, dtype=jnp.float32) for t in (cos_d, sin_d, cos_m, sin_m))


def _spread_rope_cols(w):
    z = jnp.zeros(w.shape[:-1] + (ROPE_HALF,), w.dtype)
    return jnp.concatenate([w[..., :ROPE_HALF], z, w[..., ROPE_HALF:], z], axis=-1)


def _prepare_weights(w_in, w_q_b, w_kv_b, w_o, w_up, w_down):
    bf = jnp.bfloat16
    n_lora = Q_LORA + KV_LORA
    wm = jnp.concatenate([w_in[:, :n_lora], _spread_rope_cols(w_in[:, n_lora:n_lora + ROPE_DIM])],
                         axis=-1).astype(bf)
    w_dil = w_in[:, n_lora + ROPE_DIM:].astype(bf)
    wq = w_q_b.reshape(Q_LORA, MLA_HEADS, NOPE_DIM + ROPE_DIM)
    wq = jnp.concatenate([wq[..., :NOPE_DIM], _spread_rope_cols(wq[..., NOPE_DIM:])], axis=-1)
    wq = wq.reshape(Q_LORA, MLA_HEADS * MLA_QK).astype(bf)
    wkv = w_kv_b.reshape(KV_LORA, MLA_HEADS, NOPE_DIM + V_DIM)
    wk = wkv[..., :NOPE_DIM].reshape(KV_LORA, MLA_HEADS * NOPE_DIM).astype(bf)
    wv = wkv[..., NOPE_DIM:].reshape(KV_LORA, MLA_WIDTH).astype(bf)
    return wm, w_dil, wq, wk, wv, w_o.astype(bf), w_up.astype(bf), w_down.astype(bf)


def _trunk(x, gains, weights):
    attn_g, qa_g, kva_g, mla_g, dil_g, mlp_g, final_g = gains
    wm, w_dil, wq, wk, wv, wo, wu, wd = weights
    batch, seq, _ = x.shape
    x2d = x.reshape(batch * seq, D_MODEL)
    cos_d, sin_d, cos_m, sin_m = _rope_tables(seq)

    q, k, v = _mla_pre(x2d, seq, attn_g, wm, qa_g, kva_g, wq, wk, wv, cos_m, sin_m)
    qkv_d = _dil_proj(x2d, batch, seq, attn_g, w_dil, cos_d, sin_d)
    o_a = _mla_attn(q.reshape(batch, seq, -1), k.reshape(batch, seq, -1), v.reshape(batch, seq, -1))
    o_b = _dil_attn(qkv_d)
    x1 = _mix_out(x2d, o_a.reshape(batch * seq, MLA_WIDTH), o_b.reshape(batch * seq, DIL_WIDTH),
                  mla_g, dil_g, wo)
    y = _mlp(x1, mlp_g, wu, wd, final_g)
    return y.reshape(batch, seq, D_MODEL)


def kernel(x_prompt, x_sample, attn_norm_g, w_in, q_a_norm_g, w_q_b, kv_a_norm_g, w_kv_b,
           mla_out_norm_g, dil_out_norm_g, w_o, mlp_norm_g, w_up, w_down, final_norm_g):
    assert w_in.shape[0] == 1, "single-layer block"
    weights = _prepare_weights(w_in[0], w_q_b[0], w_kv_b[0], w_o[0], w_up[0], w_down[0])
    gains = (attn_norm_g[0][None], q_a_norm_g[0][None], kv_a_norm_g[0][None],
             mla_out_norm_g[0][None], dil_out_norm_g[0][None], mlp_norm_g[0][None],
             final_norm_g[None])
    return (_trunk(x_prompt, gains, weights), _trunk(x_sample, gains, weights))
```

```python
import functools
import math

import numpy as np
import jax
import jax.numpy as jnp
from jax import lax
from jax.experimental import pallas as pl
from jax.experimental.pallas import tpu as pltpu

D_MODEL = 2048
EPS = 1e-6
ROPE_THETA = 10000.0
MLA_HEADS = 8
Q_LORA = 512
KV_LORA = 512
NOPE_DIM = 128
ROPE_DIM = 64
V_DIM = 128
MLA_WIDTH = MLA_HEADS * V_DIM
DIL_HEADS = 8
DIL_HD = 128
DIL_WIDTH = DIL_HEADS * DIL_HD
DIL_PATTERNS = ((128, 1), (512, 4), (2048, 16))
D_FF = 4 * D_MODEL

LANES = 128
VMEM_LIMIT_BYTES = 56 * 1024 * 1024
ROW_CHUNK = 128

MLA_QK = NOPE_DIM + LANES
ROPE_HALF = ROPE_DIM // 2
MASK_VALUE = -1e30

MLA_Q_SCALE = math.log2(math.e) / math.sqrt(NOPE_DIM + ROPE_DIM)
DIL_Q_SCALE = math.log2(math.e) / math.sqrt(DIL_HD)

_NT = (((1,), (1,)), ((), ()))


def _rms(x, g):
    return x * lax.rsqrt(jnp.mean(x * x, axis=-1, keepdims=True) + EPS) * g


def _rope128(x, cos, sin):
    return x * cos + pltpu.roll(x, LANES // 2, 1) * sin


def _bf16_dot(a, b):
    return jnp.dot(a, b, preferred_element_type=jnp.float32)


def _params(*sem):
    return pltpu.CompilerParams(dimension_semantics=sem, vmem_limit_bytes=VMEM_LIMIT_BYTES)


def _mla_pre_kernel(x_ref, g_ref, wm_ref, qg_ref, kvg_ref, wq_ref, wk_ref, wv_ref,
                    cos_ref, sin_ref, q_out, k_out, v_out):
    h = _rms(x_ref[...], g_ref[...]).astype(jnp.bfloat16)
    a = _bf16_dot(h, wm_ref[...])
    qn = _rms(a[:, :Q_LORA], qg_ref[...]).astype(jnp.bfloat16)
    kvn = _rms(a[:, Q_LORA:Q_LORA + KV_LORA], kvg_ref[...]).astype(jnp.bfloat16)
    cos, sin = cos_ref[...], sin_ref[...]
    k_rope = _rope128(a[:, Q_LORA + KV_LORA:], cos, sin).astype(jnp.bfloat16)
    q = _bf16_dot(qn, wq_ref[...]) * MLA_Q_SCALE
    k_nope = _bf16_dot(kvn, wk_ref[...])
    for hd in range(MLA_HEADS):
        lo = hd * MLA_QK
        q_out[:, lo:lo + NOPE_DIM] = q[:, lo:lo + NOPE_DIM].astype(jnp.bfloat16)
        q_out[:, lo + NOPE_DIM:lo + MLA_QK] = _rope128(
            q[:, lo + NOPE_DIM:lo + MLA_QK], cos, sin).astype(jnp.bfloat16)
        k_out[:, lo:lo + NOPE_DIM] = k_nope[:, hd * NOPE_DIM:(hd + 1) * NOPE_DIM].astype(jnp.bfloat16)
        k_out[:, lo + NOPE_DIM:lo + MLA_QK] = k_rope
    v_out[...] = _bf16_dot(kvn, wv_ref[...]).astype(jnp.bfloat16)


def _resident(shape):
    return pl.BlockSpec(shape, lambda *_: (0,) * len(shape), pipeline_mode=pl.Buffered(1))


def _mla_pre(x2d, seq, g, wm, qg, kvg, wq, wk, wv, cos_m, sin_m, tm=512):
    t = x2d.shape[0]
    pos_blocks = seq // tm
    const = lambda i: (0, 0)
    row = lambda i: (i, 0)
    pos = lambda i: (i % pos_blocks, 0)
    return pl.pallas_call(
        _mla_pre_kernel,
        grid=(t // tm,),
        in_specs=[
            pl.BlockSpec((tm, D_MODEL), row),
            pl.BlockSpec((1, D_MODEL), const),
            _resident(wm.shape),
            pl.BlockSpec((1, Q_LORA), const),
            pl.BlockSpec((1, KV_LORA), const),
            _resident(wq.shape),
            _resident(wk.shape),
            _resident(wv.shape),
            pl.BlockSpec((tm, LANES), pos),
            pl.BlockSpec((tm, LANES), pos),
        ],
        out_specs=[
            pl.BlockSpec((tm, MLA_HEADS * MLA_QK), row),
            pl.BlockSpec((tm, MLA_HEADS * MLA_QK), row),
            pl.BlockSpec((tm, MLA_WIDTH), row),
        ],
        out_shape=[
            jax.ShapeDtypeStruct((t, MLA_HEADS * MLA_QK), jnp.bfloat16),
            jax.ShapeDtypeStruct((t, MLA_HEADS * MLA_QK), jnp.bfloat16),
            jax.ShapeDtypeStruct((t, MLA_WIDTH), jnp.bfloat16),
        ],
        compiler_params=_params("parallel"),
        name="mla_pre",
    )(x2d, g, wm, qg, kvg, wq, wk, wv, cos_m, sin_m)


def _dil_proj_kernel(x_ref, g_ref, w_ref, cos_ref, sin_ref, *refs):
    n = len(DIL_PATTERNS)
    out_refs, planes = refs[:n], refs[n:]
    dils = [d for _, d in DIL_PATTERNS]
    tm = x_ref.shape[0]
    h = _rms(x_ref[...], g_ref[...]).astype(jnp.bfloat16)
    cos, sin = cos_ref[...], sin_ref[...]
    for which in range(3):
        y = _bf16_dot(h, w_ref[:, which * DIL_WIDTH:(which + 1) * DIL_WIDTH])
        for hd in range(DIL_HEADS):
            lanes = slice(hd * DIL_HD, (hd + 1) * DIL_HD)
            yh = y[:, lanes]
            if which < 2:
                yh = _rope128(yh, cos, sin)
            if which == 0:
                yh = yh * DIL_Q_SCALE
            out_refs[0][which, 0, 0, :, lanes] = yh.astype(jnp.bfloat16)
            planes[0][hd] = yh
            for lvl in range(1, n):
                d_prev, d = dils[lvl - 1], dils[lvl]
                ratio, rows_prev, rows = d // d_prev, tm // d_prev, tm // d
                for r_prev in range(d_prev):
                    for sub in range(ratio):
                        c = planes[lvl - 1][hd, pl.ds(r_prev * rows_prev + sub, rows, stride=ratio), :]
                        cls = d_prev * sub + r_prev
                        out_refs[lvl][which, 0, cls, :, lanes] = c.astype(jnp.bfloat16)
                        if lvl + 1 < n:
                            planes[lvl][hd, cls * rows:(cls + 1) * rows, :] = c


def _dil_proj(x2d, batch, seq, g, w_dil, cos_d, sin_d, tm=512):
    t = x2d.shape[0]
    tiles = seq // tm
    dils = [d for _, d in DIL_PATTERNS]
    assert dils[0] == 1 and all(b % a == 0 for a, b in zip(dils, dils[1:]))
    const = lambda i: (0, 0)
    pos = lambda i: (i % tiles, 0)
    return pl.pallas_call(
        _dil_proj_kernel,
        grid=(t // tm,),
        in_specs=[
            pl.BlockSpec((tm, D_MODEL), lambda i: (i, 0)),
            pl.BlockSpec((1, D_MODEL), const),
            _resident(w_dil.shape),
            pl.BlockSpec((tm, LANES), pos),
            pl.BlockSpec((tm, LANES), pos),
        ],
        out_specs=[
            pl.BlockSpec((3, 1, d, tm // d, DIL_WIDTH), lambda i: (0, i // tiles, 0, i % tiles, 0))
            for d in dils
        ],
        out_shape=[jax.ShapeDtypeStruct((3, batch, d, seq // d, DIL_WIDTH), jnp.bfloat16) for d in dils],
        scratch_shapes=[pltpu.VMEM((DIL_HEADS, tm, DIL_HD), jnp.float32) for _ in dils[:-1]],
        compiler_params=_params("parallel"),
        name="dil_proj",
    )(x2d, g, w_dil, cos_d, sin_d)


def _mla_attn_kernel(q_ref, k_ref, v_ref, o_ref, vext_scr, *, chunk):
    heads = vext_scr.shape[0]

    @pl.when(pl.program_id(2) == 0)
    def _():
        for hd in range(heads):
            vext_scr[hd, :, :V_DIM] = v_ref[0, :, hd * V_DIM:(hd + 1) * V_DIM]
            vext_scr[hd, :, V_DIM:] = jnp.ones((vext_scr.shape[1], V_DIM), jnp.bfloat16)

    for hd in range(heads):
        q = q_ref[0, :, hd * MLA_QK:(hd + 1) * MLA_QK]
        m = acc = None
        for c in range(k_ref.shape[1] // chunk):
            rows = slice(c * chunk, (c + 1) * chunk)
            s = lax.dot_general(q, k_ref[0, rows, hd * MLA_QK:(hd + 1) * MLA_QK], _NT,
                                preferred_element_type=jnp.float32)
            m_c = jnp.max(s, axis=-1, keepdims=True)
            m_new = m_c if m is None else jnp.maximum(m, m_c)
            p = jnp.exp2((s - m_new).astype(jnp.bfloat16))
            pv = _bf16_dot(p, vext_scr[hd, rows, :])
            acc = pv if m is None else acc * jnp.exp2(m - m_new) + pv
            m = m_new
        o = acc[:, :V_DIM] / acc[:, V_DIM:V_DIM + 1]
        o_ref[0, :, hd * V_DIM:(hd + 1) * V_DIM] = o.astype(o_ref.dtype)


def _mla_attn(q, k, v, scores_per_step=4 * 1024 * 1024, chunk=512, heads=2):
    b, s, _ = q.shape
    tq = min(s, scores_per_step // s)
    return pl.pallas_call(
        functools.partial(_mla_attn_kernel, chunk=chunk),
        grid=(b, MLA_HEADS // heads, s // tq),
        in_specs=[
            pl.BlockSpec((1, tq, heads * MLA_QK), lambda bi, h, i: (bi, i, h)),
            pl.BlockSpec((1, s, heads * MLA_QK), lambda bi, h, i: (bi, 0, h)),
            pl.BlockSpec((1, s, heads * V_DIM), lambda bi, h, i: (bi, 0, h)),
        ],
        out_specs=pl.BlockSpec((1, tq, heads * V_DIM), lambda bi, h, i: (bi, i, h)),
        out_shape=jax.ShapeDtypeStruct((b, s, MLA_WIDTH), jnp.bfloat16),
        scratch_shapes=[pltpu.VMEM((heads, s, 2 * V_DIM), jnp.bfloat16)],
        compiler_params=_params("parallel", "parallel", "arbitrary"),
        name="mla_attn",
    )(q, k, v)


def _dil_attn_kernel(*refs, block, sub, seq):
    n = len(DIL_PATTERNS)
    in_refs, o_ref, o_nat, l_nat = refs[:7 * n], refs[7 * n], refs[7 * n + 1], refs[7 * n + 2]
    i = pl.program_id(1)
    heads = o_nat.shape[1]
    for p, (window, dilation) in enumerate(DIL_PATTERNS):
        q_ref, kp_ref, kc_ref, kn_ref, vp_ref, vc_ref, vn_ref = in_refs[7 * p:7 * p + 7]
        half = window // (2 * dilation)
        rows_cls = block // dilation
        sub_p = min(sub, rows_cls)
        nk = sub_p + 2 * half
        rel = (lax.broadcasted_iota(jnp.int32, (sub_p, nk), 1) - half
               - lax.broadcasted_iota(jnp.int32, (sub_p, nk), 0))
        in_band = jnp.where(rel <= half, jnp.where(rel >= -half, 1, 0), 0)
        ones = jnp.ones((nk, DIL_HD), jnp.bfloat16)
        biases = []
        for st in range(rows_cls // sub_p):
            k_pos = i * rows_cls + st * sub_p - half + lax.broadcasted_iota(jnp.int32, (sub_p, nk), 1)
            valid = jnp.where(k_pos >= 0, jnp.where(k_pos < seq // dilation, in_band, 0), 0)
            biases.append(jnp.where(valid > 0, 0.0, MASK_VALUE).astype(jnp.float32))
        for cl in range(dilation):
            k = jnp.concatenate([kp_ref[0, 0, cl], kc_ref[0, 0, cl], kn_ref[0, 0, cl]], axis=0)
            v = jnp.concatenate([vp_ref[0, 0, cl], vc_ref[0, 0, cl], vn_ref[0, 0, cl]], axis=0)
            for st in range(rows_cls // sub_p):
                q = q_ref[0, 0, cl, st * sub_p:(st + 1) * sub_p, :]
                if dilation > 1:
                    tok = pl.ds(cl + dilation * st * sub_p, sub_p, stride=dilation)
                else:
                    tok = slice(st * sub_p, (st + 1) * sub_p)
                for hd in range(heads):
                    lanes = slice(hd * DIL_HD, (hd + 1) * DIL_HD)
                    kh = k[st * sub_p:st * sub_p + nk, lanes]
                    vh = v[st * sub_p:st * sub_p + nk, lanes]
                    s = lax.dot_general(q[:, lanes], kh, _NT, preferred_element_type=jnp.float32) + biases[st]
                    m = jnp.max(s, axis=-1, keepdims=True)
                    e = jnp.exp2((s - m).astype(jnp.bfloat16))
                    pv = _bf16_dot(e, jnp.concatenate([vh, ones], axis=1))
                    den = pv[:, DIL_HD:]
                    o_nat[p, hd, tok, :] = pv[:, :DIL_HD] / den
                    l_nat[p, hd, tok, :] = m + jnp.log2(den)

    for hd in range(heads):
        for c in range(block // ROW_CHUNK):
            rows = slice(c * ROW_CHUNK, (c + 1) * ROW_CHUNK)
            lses = [l_nat[p, hd, rows, :] for p in range(n)]
            mx = functools.reduce(jnp.maximum, lses)
            es = [jnp.exp2(l - mx) for l in lses]
            num = functools.reduce(jnp.add, [e * o_nat[p, hd, rows, :] for p, e in enumerate(es)])
            o_ref[0, rows, hd * DIL_HD:(hd + 1) * DIL_HD] = num / functools.reduce(jnp.add, es)


def _dil_attn(qkv_by_pattern, block=2048, sub=128, heads=2):
    _, batch, _, seq, _ = qkv_by_pattern[0].shape
    block = min(block, seq)
    width = heads * DIL_HD
    args, in_specs = [], []
    for qkv, (window, dilation) in zip(qkv_by_pattern, DIL_PATTERNS):
        half = window // (2 * dilation)
        rows_cls = block // dilation
        per_half = rows_cls // half
        last = seq // dilation // half - 1

        def spec(which, rows, row_block, dilation=dilation):
            return pl.BlockSpec((1, 1, dilation, rows, width),
                                lambda b, i, g: (which, b, 0, row_block(i), g))

        cur = lambda i: i
        prev = lambda i, per_half=per_half: jnp.maximum(i * per_half - 1, 0)
        nxt = lambda i, per_half=per_half, last=last: jnp.minimum((i + 1) * per_half, last)
        in_specs += [
            spec(0, rows_cls, cur),
            spec(1, half, prev), spec(1, rows_cls, cur), spec(1, half, nxt),
            spec(2, half, prev), spec(2, rows_cls, cur), spec(2, half, nxt),
        ]
        args += [qkv] * 7
    n = len(DIL_PATTERNS)
    return pl.pallas_call(
        functools.partial(_dil_attn_kernel, block=block, sub=sub, seq=seq),
        grid=(batch, seq // block, DIL_HEADS // heads),
        in_specs=in_specs,
        out_specs=pl.BlockSpec((1, block, width), lambda b, i, g: (b, i, g)),
        out_shape=jax.ShapeDtypeStruct((batch, seq, DIL_WIDTH), jnp.float32),
        scratch_shapes=[
            pltpu.VMEM((n, heads, block, DIL_HD), jnp.float32),
            pltpu.VMEM((n, heads, block, DIL_HD), jnp.float32),
        ],
        compiler_params=_params("parallel", "parallel", "parallel"),
        name="dil_attn",
    )(*args)


def _mix_out_kernel(x_ref, oa_ref, ob_ref, ga_ref, gb_ref, wo_ref, y_ref):
    for c in range(x_ref.shape[0] // ROW_CHUNK):
        rows = slice(c * ROW_CHUNK, (c + 1) * ROW_CHUNK)
        mix_a = _rms(oa_ref[rows, :].astype(jnp.float32), ga_ref[...]).astype(jnp.bfloat16)
        mix_b = _rms(ob_ref[rows, :], gb_ref[...]).astype(jnp.bfloat16)
        y = _bf16_dot(mix_a, wo_ref[:MLA_WIDTH, :]) + _bf16_dot(mix_b, wo_ref[MLA_WIDTH:, :])
        y_ref[rows, :] = x_ref[rows, :] + y


def _mix_out(x2d, o_a, o_b, ga, gb, wo, tm=512):
    t = x2d.shape[0]
    row = lambda i: (i, 0)
    const = lambda i: (0, 0)
    return pl.pallas_call(
        _mix_out_kernel,
        grid=(t // tm,),
        in_specs=[
            pl.BlockSpec((tm, D_MODEL), row),
            pl.BlockSpec((tm, MLA_WIDTH), row),
            pl.BlockSpec((tm, DIL_WIDTH), row),
            pl.BlockSpec((1, MLA_WIDTH), const),
            pl.BlockSpec((1, DIL_WIDTH), const),
            _resident(wo.shape),
        ],
        out_specs=pl.BlockSpec((tm, D_MODEL), row),
        out_shape=jax.ShapeDtypeStruct((t, D_MODEL), jnp.float32),
        compiler_params=_params("parallel"),
        name="mix_out",
    )(x2d, o_a, o_b, ga, gb, wo)


def _mlp_kernel(x_ref, g_ref, wu_ref, wd_ref, gf_ref, y_ref, h_scr, acc_scr):
    k = pl.program_id(1)

    @pl.when(k == 0)
    def _():
        x = x_ref[...]
        h_scr[...] = _rms(x, g_ref[...]).astype(jnp.bfloat16)
        acc_scr[...] = x

    u = _bf16_dot(h_scr[...], wu_ref[...])
    a = jnp.square(jnp.maximum(u, 0.0)).astype(jnp.bfloat16)
    acc_scr[...] += _bf16_dot(a, wd_ref[...])

    @pl.when(k == pl.num_programs(1) - 1)
    def _():
        y_ref[...] = _rms(acc_scr[...], gf_ref[...])


def _mlp(x2d, g, wu, wd, gf, tm=512, tf=1024):
    t = x2d.shape[0]
    return pl.pallas_call(
        _mlp_kernel,
        grid=(t // tm, D_FF // tf),
        in_specs=[
            pl.BlockSpec((tm, D_MODEL), lambda i, k: (i, 0)),
            pl.BlockSpec((1, D_MODEL), lambda i, k: (0, 0)),
            pl.BlockSpec((D_MODEL, tf), lambda i, k: (0, k)),
            pl.BlockSpec((tf, D_MODEL), lambda i, k: (k, 0)),
            pl.BlockSpec((1, D_MODEL), lambda i, k: (0, 0)),
        ],
        out_specs=pl.BlockSpec((tm, D_MODEL), lambda i, k: (i, 0)),
        out_shape=jax.ShapeDtypeStruct((t, D_MODEL), jnp.float32),
        scratch_shapes=[
            pltpu.VMEM((tm, D_MODEL), jnp.bfloat16),
            pltpu.VMEM((tm, D_MODEL), jnp.float32),
        ],
        compiler_params=_params("parallel", "arbitrary"),
        name="mlp",
    )(x2d, g, wu, wd, gf)


def _rope_tables(seq):
    pos = np.arange(seq, dtype=np.float64)

    def cos_sin(dim):
        inv = 1.0 / (ROPE_THETA ** (np.arange(0, dim, 2, dtype=np.float64) / dim))
        ang = pos[:, None] * inv[None, :]
        return np.cos(ang), np.sin(ang)

    c, s = cos_sin(DIL_HD)
    cos_d = np.concatenate([c, c], axis=-1)
    sin_d = np.concatenate([-s, s], axis=-1)
    c, s = cos_sin(ROPE_DIM)
    z = np.zeros_like(c)
    cos_m = np.concatenate([c, z, c, z], axis=-1)
    sin_m = np.concatenate([-s, z, s, z], axis=-1)
    return tuple(jnp.asarray(t, dtype=jnp.float32) for t in (cos_d, sin_d, cos_m, sin_m))


def _spread_rope_cols(w):
    z = jnp.zeros(w.shape[:-1] + (ROPE_HALF,), w.dtype)
    return jnp.concatenate([w[..., :ROPE_HALF], z, w[..., ROPE_HALF:], z], axis=-1)


def _prepare_weights(w_in, w_q_b, w_kv_b, w_o, w_up, w_down):
    bf = jnp.bfloat16
    n_lora = Q_LORA + KV_LORA
    wm = jnp.concatenate([w_in[:, :n_lora], _spread_rope_cols(w_in[:, n_lora:n_lora + ROPE_DIM])],
                         axis=-1).astype(bf)
    w_dil = w_in[:, n_lora + ROPE_DIM:].astype(bf)
    wq = w_q_b.reshape(Q_LORA, MLA_HEADS, NOPE_DIM + ROPE_DIM)
    wq = jnp.concatenate([wq[..., :NOPE_DIM], _spread_rope_cols(wq[..., NOPE_DIM:])], axis=-1)
    wq = wq.reshape(Q_LORA, MLA_HEADS * MLA_QK).astype(bf)
    wkv = w_kv_b.reshape(KV_LORA, MLA_HEADS, NOPE_DIM + V_DIM)
    wk = wkv[..., :NOPE_DIM].reshape(KV_LORA, MLA_HEADS * NOPE_DIM).astype(bf)
    wv = wkv[..., NOPE_DIM:].reshape(KV_LORA, MLA_WIDTH).astype(bf)
    return wm, w_dil, wq, wk, wv, w_o.astype(bf), w_up.astype(bf), w_down.astype(bf)


def _trunk(x, gains, weights):
    attn_g, qa_g, kva_g, mla_g, dil_g, mlp_g, final_g = gains
    wm, w_dil, wq, wk, wv, wo, wu, wd = weights
    batch, seq, _ = x.shape
    x2d = x.reshape(batch * seq, D_MODEL)
    cos_d, sin_d, cos_m, sin_m = _rope_tables(seq)

    q, k, v = _mla_pre(x2d, seq, attn_g, wm, qa_g, kva_g, wq, wk, wv, cos_m, sin_m)
    qkv_d = _dil_proj(x2d, batch, seq, attn_g, w_dil, cos_d, sin_d)
    o_a = _mla_attn(q.reshape(batch, seq, -1), k.reshape(batch, seq, -1), v.reshape(batch, seq, -1))
    o_b = _dil_attn(qkv_d)
    x1 = _mix_out(x2d, o_a.reshape(batch * seq, MLA_WIDTH), o_b.reshape(batch * seq, DIL_WIDTH),
                  mla_g, dil_g, wo)
    y = _mlp(x1, mlp_g, wu, wd, final_g)
    return y.reshape(batch, seq, D_MODEL)


def kernel(x_prompt, x_sample, attn_norm_g, w_in, q_a_norm_g, w_q_b, kv_a_norm_g, w_kv_b,
           mla_out_norm_g, dil_out_norm_g, w_o, mlp_norm_g, w_up, w_down, final_norm_g):
    assert w_in.shape[0] == 1, "single-layer block"
    weights = _prepare_weights(w_in[0], w_q_b[0], w_kv_b[0], w_o[0], w_up[0], w_down[0])
    gains = (attn_norm_g[0][None], q_a_norm_g[0][None], kv_a_norm_g[0][None],
             mla_out_norm_g[0][None], dil_out_norm_g[0][None], mlp_norm_g[0][None],
             final_norm_g[None])
    return (_trunk(x_prompt, gains, weights), _trunk(x_sample, gains, weights))
```

```python
import functools
import math

import numpy as np
import jax
import jax.numpy as jnp
from jax import lax
from jax.experimental import pallas as pl
from jax.experimental.pallas import tpu as pltpu

D_MODEL = 2048
EPS = 1e-6
ROPE_THETA = 10000.0
MLA_HEADS = 8
Q_LORA = 512
KV_LORA = 512
NOPE_DIM = 128
ROPE_DIM = 64
V_DIM = 128
MLA_WIDTH = MLA_HEADS * V_DIM
DIL_HEADS = 8
DIL_HD = 128
DIL_WIDTH = DIL_HEADS * DIL_HD
IN_DIL_COLS = 3 * DIL_WIDTH
DIL_PATTERNS = ((128, 1), (512, 4), (2048, 16))
D_FF = 4 * D_MODEL

LANES = 128
VMEM_LIMIT_BYTES = 56 * 1024 * 1024
ROW_CHUNK = 128

MLA_QK = NOPE_DIM + LANES
ROPE_HALF = ROPE_DIM // 2
MASK_VALUE = -1e30

MLA_Q_SCALE = math.log2(math.e) / math.sqrt(NOPE_DIM + ROPE_DIM)
DIL_Q_SCALE = math.log2(math.e) / math.sqrt(DIL_HD)

_NT = (((1,), (1,)), ((), ()))


def _rms(x, g):
    return x * lax.rsqrt(jnp.mean(x * x, axis=-1, keepdims=True) + EPS) * g


def _rope128(x, cos, sin):
    return x * cos + pltpu.roll(x, LANES // 2, 1) * sin


def _bf16_dot(a, b):
    return jnp.dot(a, b, preferred_element_type=jnp.float32)


def _params(*sem):
    return pltpu.CompilerParams(dimension_semantics=sem, vmem_limit_bytes=VMEM_LIMIT_BYTES)


def _mla_pre_kernel(x_ref, g_ref, wa_ref, wr_ref, qg_ref, kvg_ref, wq_ref, wk_ref, wv_ref,
                    cos_ref, sin_ref, q_out, k_out, v_out):
    h = _rms(x_ref[...], g_ref[...]).astype(jnp.bfloat16)
    a = _bf16_dot(h, wa_ref[...])
    qn = _rms(a[:, :Q_LORA], qg_ref[...]).astype(jnp.bfloat16)
    kvn = _rms(a[:, Q_LORA:], kvg_ref[...]).astype(jnp.bfloat16)
    cos, sin = cos_ref[...], sin_ref[...]
    k_rope = _rope128(_bf16_dot(h, wr_ref[...]), cos, sin).astype(jnp.bfloat16)
    q = _bf16_dot(qn, wq_ref[...]) * MLA_Q_SCALE
    k_nope = _bf16_dot(kvn, wk_ref[...])
    for hd in range(MLA_HEADS):
        lo = hd * MLA_QK
        q_out[:, lo:lo + NOPE_DIM] = q[:, lo:lo + NOPE_DIM].astype(jnp.bfloat16)
        q_out[:, lo + NOPE_DIM:lo + MLA_QK] = _rope128(
            q[:, lo + NOPE_DIM:lo + MLA_QK], cos, sin).astype(jnp.bfloat16)
        k_out[:, lo:lo + NOPE_DIM] = k_nope[:, hd * NOPE_DIM:(hd + 1) * NOPE_DIM].astype(jnp.bfloat16)
        k_out[:, lo + NOPE_DIM:lo + MLA_QK] = k_rope
    v_out[...] = _bf16_dot(kvn, wv_ref[...]).astype(jnp.bfloat16)


def _resident(shape, block_index=None):
    index = block_index or (0,) * len(shape)
    return pl.BlockSpec(shape, lambda *_: index, pipeline_mode=pl.Buffered(1))


def _mla_pre(x2d, seq, g, w_in_all, qg, kvg, wq, wk, wv, cos_m, sin_m, tm=512):
    t = x2d.shape[0]
    pos_blocks = seq // tm
    n_lora = Q_LORA + KV_LORA
    assert IN_DIL_COLS % n_lora == 0 and (IN_DIL_COLS + n_lora) % LANES == 0
    const = lambda i: (0, 0)
    row = lambda i: (i, 0)
    pos = lambda i: (i % pos_blocks, 0)
    return pl.pallas_call(
        _mla_pre_kernel,
        grid=(t // tm,),
        in_specs=[
            pl.BlockSpec((tm, D_MODEL), row),
            pl.BlockSpec((1, D_MODEL), const),
            _resident((D_MODEL, n_lora), (0, IN_DIL_COLS // n_lora)),
            _resident((D_MODEL, LANES), (0, (IN_DIL_COLS + n_lora) // LANES)),
            pl.BlockSpec((1, Q_LORA), const),
            pl.BlockSpec((1, KV_LORA), const),
            _resident(wq.shape),
            _resident(wk.shape),
            _resident(wv.shape),
            pl.BlockSpec((tm, LANES), pos),
            pl.BlockSpec((tm, LANES), pos),
        ],
        out_specs=[
            pl.BlockSpec((tm, MLA_HEADS * MLA_QK), row),
            pl.BlockSpec((tm, MLA_HEADS * MLA_QK), row),
            pl.BlockSpec((tm, MLA_WIDTH), row),
        ],
        out_shape=[
            jax.ShapeDtypeStruct((t, MLA_HEADS * MLA_QK), jnp.bfloat16),
            jax.ShapeDtypeStruct((t, MLA_HEADS * MLA_QK), jnp.bfloat16),
            jax.ShapeDtypeStruct((t, MLA_WIDTH), jnp.bfloat16),
        ],
        compiler_params=_params("parallel"),
        name="mla_pre",
    )(x2d, g, w_in_all, w_in_all, qg, kvg, wq, wk, wv, cos_m, sin_m)


def _dil_proj_kernel(x_ref, g_ref, w_ref, cos_ref, sin_ref, *refs):
    n = len(DIL_PATTERNS)
    out_refs, planes = refs[:n], refs[n:]
    dils = [d for _, d in DIL_PATTERNS]
    tm = x_ref.shape[0]
    h = _rms(x_ref[...], g_ref[...]).astype(jnp.bfloat16)
    cos, sin = cos_ref[...], sin_ref[...]
    for which in range(3):
        y = _bf16_dot(h, w_ref[:, which * DIL_WIDTH:(which + 1) * DIL_WIDTH])
        for hd in range(DIL_HEADS):
            lanes = slice(hd * DIL_HD, (hd + 1) * DIL_HD)
            yh = y[:, lanes]
            if which < 2:
                yh = _rope128(yh, cos, sin)
            if which == 0:
                yh = yh * DIL_Q_SCALE
            out_refs[0][which, 0, 0, :, lanes] = yh.astype(jnp.bfloat16)
            planes[0][hd] = yh
            for lvl in range(1, n):
                d_prev, d = dils[lvl - 1], dils[lvl]
                ratio, rows_prev, rows = d // d_prev, tm // d_prev, tm // d
                for r_prev in range(d_prev):
                    for sub in range(ratio):
                        c = planes[lvl - 1][hd, pl.ds(r_prev * rows_prev + sub, rows, stride=ratio), :]
                        cls = d_prev * sub + r_prev
                        out_refs[lvl][which, 0, cls, :, lanes] = c.astype(jnp.bfloat16)
                        if lvl + 1 < n:
                            planes[lvl][hd, cls * rows:(cls + 1) * rows, :] = c


def _dil_proj(x2d, batch, seq, g, w_in_all, cos_d, sin_d, tm=512):
    t = x2d.shape[0]
    tiles = seq // tm
    dils = [d for _, d in DIL_PATTERNS]
    assert dils[0] == 1 and all(b % a == 0 for a, b in zip(dils, dils[1:]))
    const = lambda i: (0, 0)
    pos = lambda i: (i % tiles, 0)
    return pl.pallas_call(
        _dil_proj_kernel,
        grid=(t // tm,),
        in_specs=[
            pl.BlockSpec((tm, D_MODEL), lambda i: (i, 0)),
            pl.BlockSpec((1, D_MODEL), const),
            _resident((D_MODEL, IN_DIL_COLS)),
            pl.BlockSpec((tm, LANES), pos),
            pl.BlockSpec((tm, LANES), pos),
        ],
        out_specs=[
            pl.BlockSpec((3, 1, d, tm // d, DIL_WIDTH), lambda i: (0, i // tiles, 0, i % tiles, 0))
            for d in dils
        ],
        out_shape=[jax.ShapeDtypeStruct((3, batch, d, seq // d, DIL_WIDTH), jnp.bfloat16) for d in dils],
        scratch_shapes=[pltpu.VMEM((DIL_HEADS, tm, DIL_HD), jnp.float32) for _ in dils[:-1]],
        compiler_params=_params("parallel"),
        name="dil_proj",
    )(x2d, g, w_in_all, cos_d, sin_d)


def _mla_attn_kernel(q_ref, k_ref, v_ref, o_ref, vext_scr, *, chunk):
    heads = vext_scr.shape[0]

    @pl.when(pl.program_id(2) == 0)
    def _():
        for hd in range(heads):
            vext_scr[hd, :, :V_DIM] = v_ref[0, :, hd * V_DIM:(hd + 1) * V_DIM]
            vext_scr[hd, :, V_DIM:] = jnp.ones((vext_scr.shape[1], V_DIM), jnp.bfloat16)

    for hd in range(heads):
        q = q_ref[0, :, hd * MLA_QK:(hd + 1) * MLA_QK]
        m = acc = None
        for c in range(k_ref.shape[1] // chunk):
            rows = slice(c * chunk, (c + 1) * chunk)
            s = lax.dot_general(q, k_ref[0, rows, hd * MLA_QK:(hd + 1) * MLA_QK], _NT,
                                preferred_element_type=jnp.float32)
            m_c = jnp.max(s, axis=-1, keepdims=True)
            m_new = m_c if m is None else jnp.maximum(m, m_c)
            p = jnp.exp2((s - m_new).astype(jnp.bfloat16))
            pv = _bf16_dot(p, vext_scr[hd, rows, :])
            acc = pv if m is None else acc * jnp.exp2(m - m_new) + pv
            m = m_new
        o = acc[:, :V_DIM] / acc[:, V_DIM:V_DIM + 1]
        o_ref[0, :, hd * V_DIM:(hd + 1) * V_DIM] = o.astype(o_ref.dtype)


def _mla_attn(q, k, v, scores_per_step=4 * 1024 * 1024, chunk=512, heads=2):
    b, s, _ = q.shape
    tq = min(s, scores_per_step // s)
    return pl.pallas_call(
        functools.partial(_mla_attn_kernel, chunk=chunk),
        grid=(b, MLA_HEADS // heads, s // tq),
        in_specs=[
            pl.BlockSpec((1, tq, heads * MLA_QK), lambda bi, h, i: (bi, i, h)),
            pl.BlockSpec((1, s, heads * MLA_QK), lambda bi, h, i: (bi, 0, h)),
            pl.BlockSpec((1, s, heads * V_DIM), lambda bi, h, i: (bi, 0, h)),
        ],
        out_specs=pl.BlockSpec((1, tq, heads * V_DIM), lambda bi, h, i: (bi, i, h)),
        out_shape=jax.ShapeDtypeStruct((b, s, MLA_WIDTH), jnp.bfloat16),
        scratch_shapes=[pltpu.VMEM((heads, s, 2 * V_DIM), jnp.bfloat16)],
        compiler_params=_params("parallel", "parallel", "arbitrary"),
        name="mla_attn",
    )(q, k, v)


def _dil_attn_kernel(*refs, block, sub, seq):
    n = len(DIL_PATTERNS)
    in_refs, o_ref, o_nat, l_nat = refs[:7 * n], refs[7 * n], refs[7 * n + 1], refs[7 * n + 2]
    i = pl.program_id(1)
    heads = o_nat.shape[1]
    for p, (window, dilation) in enumerate(DIL_PATTERNS):
        q_ref, kp_ref, kc_ref, kn_ref, vp_ref, vc_ref, vn_ref = in_refs[7 * p:7 * p + 7]
        half = window // (2 * dilation)
        rows_cls = block // dilation
        sub_p = min(sub, rows_cls)
        nk = sub_p + 2 * half
        rel = (lax.broadcasted_iota(jnp.int32, (sub_p, nk), 1) - half
               - lax.broadcasted_iota(jnp.int32, (sub_p, nk), 0))
        in_band = jnp.where(rel <= half, jnp.where(rel >= -half, 1, 0), 0)
        ones = jnp.ones((nk, DIL_HD), jnp.bfloat16)
        biases = []
        for st in range(rows_cls // sub_p):
            k_pos = i * rows_cls + st * sub_p - half + lax.broadcasted_iota(jnp.int32, (sub_p, nk), 1)
            valid = jnp.where(k_pos >= 0, jnp.where(k_pos < seq // dilation, in_band, 0), 0)
            biases.append(jnp.where(valid > 0, 0.0, MASK_VALUE).astype(jnp.float32))
        for cl in range(dilation):
            k = jnp.concatenate([kp_ref[0, 0, cl], kc_ref[0, 0, cl], kn_ref[0, 0, cl]], axis=0)
            v = jnp.concatenate([vp_ref[0, 0, cl], vc_ref[0, 0, cl], vn_ref[0, 0, cl]], axis=0)
            for st in range(rows_cls // sub_p):
                q = q_ref[0, 0, cl, st * sub_p:(st + 1) * sub_p, :]
                if dilation > 1:
                    tok = pl.ds(cl + dilation * st * sub_p, sub_p, stride=dilation)
                else:
                    tok = slice(st * sub_p, (st + 1) * sub_p)
                for hd in range(heads):
                    lanes = slice(hd * DIL_HD, (hd + 1) * DIL_HD)
                    kh = k[st * sub_p:st * sub_p + nk, lanes]
                    vh = v[st * sub_p:st * sub_p + nk, lanes]
                    s = lax.dot_general(q[:, lanes], kh, _NT, preferred_element_type=jnp.float32) + biases[st]
                    m = jnp.max(s, axis=-1, keepdims=True)
                    e = jnp.exp2((s - m).astype(jnp.bfloat16))
                    pv = _bf16_dot(e, jnp.concatenate([vh, ones], axis=1))
                    den = pv[:, DIL_HD:]
                    o_nat[p, hd, tok, :] = pv[:, :DIL_HD] / den
                    l_nat[p, hd, tok, :] = m + jnp.log2(den)

    for hd in range(heads):
        for c in range(block // ROW_CHUNK):
            rows = slice(c * ROW_CHUNK, (c + 1) * ROW_CHUNK)
            lses = [l_nat[p, hd, rows, :] for p in range(n)]
            mx = functools.reduce(jnp.maximum, lses)
            es = [jnp.exp2(l - mx) for l in lses]
            num = functools.reduce(jnp.add, [e * o_nat[p, hd, rows, :] for p, e in enumerate(es)])
            o_ref[0, rows, hd * DIL_HD:(hd + 1) * DIL_HD] = num / functools.reduce(jnp.add, es)


def _dil_attn(qkv_by_pattern, block=2048, sub=128, heads=2):
    _, batch, _, seq, _ = qkv_by_pattern[0].shape
    block = min(block, seq)
    width = heads * DIL_HD
    args, in_specs = [], []
    for qkv, (window, dilation) in zip(qkv_by_pattern, DIL_PATTERNS):
        half = window // (2 * dilation)
        rows_cls = block // dilation
        per_half = rows_cls // half
        last = seq // dilation // half - 1

        def spec(which, rows, row_block, dilation=dilation):
            return pl.BlockSpec((1, 1, dilation, rows, width),
                                lambda b, i, g: (which, b, 0, row_block(i), g))

        cur = lambda i: i
        prev = lambda i, per_half=per_half: jnp.maximum(i * per_half - 1, 0)
        nxt = lambda i, per_half=per_half, last=last: jnp.minimum((i + 1) * per_half, last)
        in_specs += [
            spec(0, rows_cls, cur),
            spec(1, half, prev), spec(1, rows_cls, cur), spec(1, half, nxt),
            spec(2, half, prev), spec(2, rows_cls, cur), spec(2, half, nxt),
        ]
        args += [qkv] * 7
    n = len(DIL_PATTERNS)
    return pl.pallas_call(
        functools.partial(_dil_attn_kernel, block=block, sub=sub, seq=seq),
        grid=(batch, seq // block, DIL_HEADS // heads),
        in_specs=in_specs,
        out_specs=pl.BlockSpec((1, block, width), lambda b, i, g: (b, i, g)),
        out_shape=jax.ShapeDtypeStruct((batch, seq, DIL_WIDTH), jnp.float32),
        scratch_shapes=[
            pltpu.VMEM((n, heads, block, DIL_HD), jnp.float32),
            pltpu.VMEM((n, heads, block, DIL_HD), jnp.float32),
        ],
        compiler_params=_params("parallel", "parallel", "parallel"),
        name="dil_attn",
    )(*args)


def _mix_out_kernel(x_ref, oa_ref, ob_ref, ga_ref, gb_ref, wo_ref, y_ref):
    for c in range(x_ref.shape[0] // ROW_CHUNK):
        rows = slice(c * ROW_CHUNK, (c + 1) * ROW_CHUNK)
        mix_a = _rms(oa_ref[rows, :].astype(jnp.float32), ga_ref[...]).astype(jnp.bfloat16)
        mix_b = _rms(ob_ref[rows, :], gb_ref[...]).astype(jnp.bfloat16)
        y = _bf16_dot(mix_a, wo_ref[:MLA_WIDTH, :]) + _bf16_dot(mix_b, wo_ref[MLA_WIDTH:, :])
        y_ref[rows, :] = x_ref[rows, :] + y


def _mix_out(x2d, o_a, o_b, ga, gb, wo, tm=512):
    t = x2d.shape[0]
    row = lambda i: (i, 0)
    const = lambda i: (0, 0)
    return pl.pallas_call(
        _mix_out_kernel,
        grid=(t // tm,),
        in_specs=[
            pl.BlockSpec((tm, D_MODEL), row),
            pl.BlockSpec((tm, MLA_WIDTH), row),
            pl.BlockSpec((tm, DIL_WIDTH), row),
            pl.BlockSpec((1, MLA_WIDTH), const),
            pl.BlockSpec((1, DIL_WIDTH), const),
            _resident(wo.shape),
        ],
        out_specs=pl.BlockSpec((tm, D_MODEL), row),
        out_shape=jax.ShapeDtypeStruct((t, D_MODEL), jnp.float32),
        compiler_params=_params("parallel"),
        name="mix_out",
    )(x2d, o_a, o_b, ga, gb, wo)


def _mlp_kernel(x_ref, g_ref, wu_ref, wd_ref, gf_ref, y_ref, h_scr, acc_scr):
    k = pl.program_id(1)

    @pl.when(k == 0)
    def _():
        x = x_ref[...]
        h_scr[...] = _rms(x, g_ref[...]).astype(jnp.bfloat16)
        acc_scr[...] = x

    u = _bf16_dot(h_scr[...], wu_ref[...])
    a = jnp.square(jnp.maximum(u, 0.0)).astype(jnp.bfloat16)
    acc_scr[...] += _bf16_dot(a, wd_ref[...])

    @pl.when(k == pl.num_programs(1) - 1)
    def _():
        y_ref[...] = _rms(acc_scr[...], gf_ref[...])


def _mlp(x2d, g, wu, wd, gf, tm=512, tf=1024):
    t = x2d.shape[0]
    return pl.pallas_call(
        _mlp_kernel,
        grid=(t // tm, D_FF // tf),
        in_specs=[
            pl.BlockSpec((tm, D_MODEL), lambda i, k: (i, 0)),
            pl.BlockSpec((1, D_MODEL), lambda i, k: (0, 0)),
            pl.BlockSpec((D_MODEL, tf), lambda i, k: (0, k)),
            pl.BlockSpec((tf, D_MODEL), lambda i, k: (k, 0)),
            pl.BlockSpec((1, D_MODEL), lambda i, k: (0, 0)),
        ],
        out_specs=pl.BlockSpec((tm, D_MODEL), lambda i, k: (i, 0)),
        out_shape=jax.ShapeDtypeStruct((t, D_MODEL), jnp.float32),
        scratch_shapes=[
            pltpu.VMEM((tm, D_MODEL), jnp.bfloat16),
            pltpu.VMEM((tm, D_MODEL), jnp.float32),
        ],
        compiler_params=_params("parallel", "arbitrary"),
        name="mlp",
    )(x2d, g, wu, wd, gf)


def _rope_tables(seq):
    pos = np.arange(seq, dtype=np.float64)

    def cos_sin(dim):
        inv = 1.0 / (ROPE_THETA ** (np.arange(0, dim, 2, dtype=np.float64) / dim))
        ang = pos[:, None] * inv[None, :]
        return np.cos(ang), np.sin(ang)

    c, s = cos_sin(DIL_HD)
    cos_d = np.concatenate([c, c], axis=-1)
    sin_d = np.concatenate([-s, s], axis=-1)
    c, s = cos_sin(ROPE_DIM)
    z = np.zeros_like(c)
    cos_m = np.concatenate([c, z, c, z], axis=-1)
    sin_m = np.concatenate([-s, z, s, z], axis=-1)
    return tuple(jnp.asarray(t, dtype=jnp.float32) for t in (cos_d, sin_d, cos_m, sin_m))


def _spread_rope_cols(w):
    z = jnp.zeros(w.shape[:-1] + (ROPE_HALF,), w.dtype)
    return jnp.concatenate([w[..., :ROPE_HALF], z, w[..., ROPE_HALF:], z], axis=-1)


def _prepare_weights(w_in, w_q_b, w_kv_b, w_o, w_up, w_down):
    bf = jnp.bfloat16
    n_lora = Q_LORA + KV_LORA
    w_in_all = jnp.concatenate(
        [w_in[:, n_lora + ROPE_DIM:], w_in[:, :n_lora], _spread_rope_cols(w_in[:, n_lora:n_lora + ROPE_DIM])],
        axis=-1).astype(bf)
    wq = w_q_b.reshape(Q_LORA, MLA_HEADS, NOPE_DIM + ROPE_DIM)
    wq = jnp.concatenate([wq[..., :NOPE_DIM], _spread_rope_cols(wq[..., NOPE_DIM:])], axis=-1)
    wq = wq.reshape(Q_LORA, MLA_HEADS * MLA_QK).astype(bf)
    wkv = w_kv_b.reshape(KV_LORA, MLA_HEADS, NOPE_DIM + V_DIM)
    wk = wkv[..., :NOPE_DIM].reshape(KV_LORA, MLA_HEADS * NOPE_DIM).astype(bf)
    wv = wkv[..., NOPE_DIM:].reshape(KV_LORA, MLA_WIDTH).astype(bf)
    return w_in_all, wq, wk, wv, w_o.astype(bf), w_up.astype(bf), w_down.astype(bf)


def _trunk(x, gains, weights):
    attn_g, qa_g, kva_g, mla_g, dil_g, mlp_g, final_g = gains
    w_in_all, wq, wk, wv, wo, wu, wd = weights
    batch, seq, _ = x.shape
    x2d = x.reshape(batch * seq, D_MODEL)
    cos_d, sin_d, cos_m, sin_m = _rope_tables(seq)

    q, k, v = _mla_pre(x2d, seq, attn_g, w_in_all, qa_g, kva_g, wq, wk, wv, cos_m, sin_m)
    qkv_d = _dil_proj(x2d, batch, seq, attn_g, w_in_all, cos_d, sin_d)
    o_a = _mla_attn(q.reshape(batch, seq, -1), k.reshape(batch, seq, -1), v.reshape(batch, seq, -1))
    o_b = _dil_attn(qkv_d)
    x1 = _mix_out(x2d, o_a.reshape(batch * seq, MLA_WIDTH), o_b.reshape(batch * seq, DIL_WIDTH),
                  mla_g, dil_g, wo)
    y = _mlp(x1, mlp_g, wu, wd, final_g)
    return y.reshape(batch, seq, D_MODEL)


def kernel(x_prompt, x_sample, attn_norm_g, w_in, q_a_norm_g, w_q_b, kv_a_norm_g, w_kv_b,
           mla_out_norm_g, dil_out_norm_g, w_o, mlp_norm_g, w_up, w_down, final_norm_g):
    assert w_in.shape[0] == 1, "single-layer block"
    weights = _prepare_weights(w_in[0], w_q_b[0], w_kv_b[0], w_o[0], w_up[0], w_down[0])
    gains = (attn_norm_g[0][None], q_a_norm_g[0][None], kv_a_norm_g[0][None],
             mla_out_norm_g[0][None], dil_out_norm_g[0][None], mlp_norm_g[0][None],
             final_norm_g[None])
    return (_trunk(x_prompt, gains, weights), _trunk(x_sample, gains, weights))
```

```python
import functools
import math

import numpy as np
import jax
import jax.numpy as jnp
from jax import lax
from jax.experimental import pallas as pl
from jax.experimental.pallas import tpu as pltpu

D_MODEL = 2048
EPS = 1e-6
ROPE_THETA = 10000.0
MLA_HEADS = 8
Q_LORA = 512
KV_LORA = 512
NOPE_DIM = 128
ROPE_DIM = 64
V_DIM = 128
MLA_WIDTH = MLA_HEADS * V_DIM
DIL_HEADS = 8
DIL_HD = 128
DIL_WIDTH = DIL_HEADS * DIL_HD
DIL_PATTERNS = ((128, 1), (512, 4), (2048, 16))
D_FF = 4 * D_MODEL

LANES = 128
VMEM_LIMIT_BYTES = 56 * 1024 * 1024
ROW_CHUNK = 128

MLA_QK = NOPE_DIM + LANES
ROPE_HALF = ROPE_DIM // 2
MASK_VALUE = -1e30

MLA_Q_SCALE = math.log2(math.e) / math.sqrt(NOPE_DIM + ROPE_DIM)
DIL_Q_SCALE = math.log2(math.e) / math.sqrt(DIL_HD)

_NT = (((1,), (1,)), ((), ()))


def _rms(x, g):
    return x * lax.rsqrt(jnp.mean(x * x, axis=-1, keepdims=True) + EPS) * g


def _rope128(x, cos, sin):
    return x * cos + pltpu.roll(x, LANES // 2, 1) * sin


def _bf16_dot(a, b):
    return jnp.dot(a, b, preferred_element_type=jnp.float32)


def _params(*sem):
    return pltpu.CompilerParams(dimension_semantics=sem, vmem_limit_bytes=VMEM_LIMIT_BYTES)


def _mla_pre_kernel(x_ref, g_ref, wm_ref, qg_ref, kvg_ref, wq_ref, wk_ref, wv_ref,
                    cos_ref, sin_ref, q_out, k_out, v_out):
    h = _rms(x_ref[...], g_ref[...]).astype(jnp.bfloat16)
    a = _bf16_dot(h, wm_ref[...])
    qn = _rms(a[:, :Q_LORA], qg_ref[...]).astype(jnp.bfloat16)
    kvn = _rms(a[:, Q_LORA:Q_LORA + KV_LORA], kvg_ref[...]).astype(jnp.bfloat16)
    cos, sin = cos_ref[...], sin_ref[...]
    k_rope = _rope128(a[:, Q_LORA + KV_LORA:], cos, sin).astype(jnp.bfloat16)
    q = _bf16_dot(qn, wq_ref[...]) * MLA_Q_SCALE
    k_nope = _bf16_dot(kvn, wk_ref[...])
    for hd in range(MLA_HEADS):
        lo = hd * MLA_QK
        q_out[:, lo:lo + NOPE_DIM] = q[:, lo:lo + NOPE_DIM].astype(jnp.bfloat16)
        q_out[:, lo + NOPE_DIM:lo + MLA_QK] = _rope128(
            q[:, lo + NOPE_DIM:lo + MLA_QK], cos, sin).astype(jnp.bfloat16)
        k_out[:, lo:lo + NOPE_DIM] = k_nope[:, hd * NOPE_DIM:(hd + 1) * NOPE_DIM].astype(jnp.bfloat16)
        k_out[:, lo + NOPE_DIM:lo + MLA_QK] = k_rope
    v_out[...] = _bf16_dot(kvn, wv_ref[...]).astype(jnp.bfloat16)


def _resident(shape):
    return pl.BlockSpec(shape, lambda *_: (0,) * len(shape), pipeline_mode=pl.Buffered(1))


def _mla_pre(x2d, seq, g, wm, qg, kvg, wq, wk, wv, cos_m, sin_m, tm=512):
    t = x2d.shape[0]
    pos_blocks = seq // tm
    const = lambda i: (0, 0)
    row = lambda i: (i, 0)
    pos = lambda i: (i % pos_blocks, 0)
    return pl.pallas_call(
        _mla_pre_kernel,
        grid=(t // tm,),
        in_specs=[
            pl.BlockSpec((tm, D_MODEL), row),
            pl.BlockSpec((1, D_MODEL), const),
            _resident(wm.shape),
            pl.BlockSpec((1, Q_LORA), const),
            pl.BlockSpec((1, KV_LORA), const),
            _resident(wq.shape),
            _resident(wk.shape),
            _resident(wv.shape),
            pl.BlockSpec((tm, LANES), pos),
            pl.BlockSpec((tm, LANES), pos),
        ],
        out_specs=[
            pl.BlockSpec((tm, MLA_HEADS * MLA_QK), row),
            pl.BlockSpec((tm, MLA_HEADS * MLA_QK), row),
            pl.BlockSpec((tm, MLA_WIDTH), row),
        ],
        out_shape=[
            jax.ShapeDtypeStruct((t, MLA_HEADS * MLA_QK), jnp.bfloat16),
            jax.ShapeDtypeStruct((t, MLA_HEADS * MLA_QK), jnp.bfloat16),
            jax.ShapeDtypeStruct((t, MLA_WIDTH), jnp.bfloat16),
        ],
        compiler_params=_params("parallel"),
        name="mla_pre",
    )(x2d, g, wm, qg, kvg, wq, wk, wv, cos_m, sin_m)


def _dil_proj_kernel(x_ref, g_ref, w_ref, cos_ref, sin_ref, *refs):
    n = len(DIL_PATTERNS)
    out_refs, planes = refs[:n], refs[n:]
    dils = [d for _, d in DIL_PATTERNS]
    tm = x_ref.shape[0]
    h = _rms(x_ref[...], g_ref[...]).astype(jnp.bfloat16)
    cos, sin = cos_ref[...], sin_ref[...]
    for which in range(3):
        y = _bf16_dot(h, w_ref[:, which * DIL_WIDTH:(which + 1) * DIL_WIDTH])
        for hd in range(DIL_HEADS):
            lanes = slice(hd * DIL_HD, (hd + 1) * DIL_HD)
            yh = y[:, lanes]
            if which < 2:
                yh = _rope128(yh, cos, sin)
            if which == 0:
                yh = yh * DIL_Q_SCALE
            out_refs[0][which, 0, 0, :, lanes] = yh.astype(jnp.bfloat16)
            planes[0][hd] = yh
            for lvl in range(1, n):
                d_prev, d = dils[lvl - 1], dils[lvl]
                ratio, rows_prev, rows = d // d_prev, tm // d_prev, tm // d
                for r_prev in range(d_prev):
                    for sub in range(ratio):
                        c = planes[lvl - 1][hd, pl.ds(r_prev * rows_prev + sub, rows, stride=ratio), :]
                        cls = d_prev * sub + r_prev
                        out_refs[lvl][which, 0, cls, :, lanes] = c.astype(jnp.bfloat16)
                        if lvl + 1 < n:
                            planes[lvl][hd, cls * rows:(cls + 1) * rows, :] = c


def _dil_proj(x2d, batch, seq, g, w_dil, cos_d, sin_d, tm=256):
    t = x2d.shape[0]
    tiles = seq // tm
    dils = [d for _, d in DIL_PATTERNS]
    assert dils[0] == 1 and all(b % a == 0 for a, b in zip(dils, dils[1:]))
    const = lambda i: (0, 0)
    pos = lambda i: (i % tiles, 0)
    return pl.pallas_call(
        _dil_proj_kernel,
        grid=(t // tm,),
        in_specs=[
            pl.BlockSpec((tm, D_MODEL), lambda i: (i, 0)),
            pl.BlockSpec((1, D_MODEL), const),
            _resident(w_dil.shape),
            pl.BlockSpec((tm, LANES), pos),
            pl.BlockSpec((tm, LANES), pos),
        ],
        out_specs=[
            pl.BlockSpec((3, 1, d, tm // d, DIL_WIDTH), lambda i: (0, i // tiles, 0, i % tiles, 0))
            for d in dils
        ],
        out_shape=[jax.ShapeDtypeStruct((3, batch, d, seq // d, DIL_WIDTH), jnp.bfloat16) for d in dils],
        scratch_shapes=[pltpu.VMEM((DIL_HEADS, tm, DIL_HD), jnp.float32) for _ in dils[:-1]],
        compiler_params=_params("parallel"),
        name="dil_proj",
    )(x2d, g, w_dil, cos_d, sin_d)


def _mla_attn_kernel(q_ref, k_ref, v_ref, o_ref, vext_scr, *, chunk):
    heads = vext_scr.shape[0]

    @pl.when(pl.program_id(2) == 0)
    def _():
        for hd in range(heads):
            vext_scr[hd, :, :V_DIM] = v_ref[0, :, hd * V_DIM:(hd + 1) * V_DIM]
            vext_scr[hd, :, V_DIM:] = jnp.ones((vext_scr.shape[1], V_DIM), jnp.bfloat16)

    for hd in range(heads):
        q = q_ref[0, :, hd * MLA_QK:(hd + 1) * MLA_QK]
        m = acc = None
        for c in range(k_ref.shape[1] // chunk):
            rows = slice(c * chunk, (c + 1) * chunk)
            s = lax.dot_general(q, k_ref[0, rows, hd * MLA_QK:(hd + 1) * MLA_QK], _NT,
                                preferred_element_type=jnp.float32)
            m_c = jnp.max(s, axis=-1, keepdims=True)
            m_new = m_c if m is None else jnp.maximum(m, m_c)
            p = jnp.exp2((s - m_new).astype(jnp.bfloat16))
            pv = _bf16_dot(p, vext_scr[hd, rows, :])
            acc = pv if m is None else acc * jnp.exp2(m - m_new) + pv
            m = m_new
        o = acc[:, :V_DIM] / acc[:, V_DIM:V_DIM + 1]
        o_ref[0, :, hd * V_DIM:(hd + 1) * V_DIM] = o.astype(o_ref.dtype)


def _mla_attn(q, k, v, scores_per_step=4 * 1024 * 1024, chunk=512, heads=2):
    b, s, _ = q.shape
    tq = min(s, scores_per_step // s)
    return pl.pallas_call(
        functools.partial(_mla_attn_kernel, chunk=chunk),
        grid=(b, MLA_HEADS // heads, s // tq),
        in_specs=[
            pl.BlockSpec((1, tq, heads * MLA_QK), lambda bi, h, i: (bi, i, h)),
            pl.BlockSpec((1, s, heads * MLA_QK), lambda bi, h, i: (bi, 0, h)),
            pl.BlockSpec((1, s, heads * V_DIM), lambda bi, h, i: (bi, 0, h)),
        ],
        out_specs=pl.BlockSpec((1, tq, heads * V_DIM), lambda bi, h, i: (bi, i, h)),
        out_shape=jax.ShapeDtypeStruct((b, s, MLA_WIDTH), jnp.bfloat16),
        scratch_shapes=[pltpu.VMEM((heads, s, 2 * V_DIM), jnp.bfloat16)],
        compiler_params=_params("parallel", "parallel", "arbitrary"),
        name="mla_attn",
    )(q, k, v)


def _dil_attn_kernel(*refs, block, sub, seq):
    n = len(DIL_PATTERNS)
    in_refs, o_ref, o_nat, l_nat = refs[:7 * n], refs[7 * n], refs[7 * n + 1], refs[7 * n + 2]
    i = pl.program_id(1)
    heads = o_nat.shape[1]
    for p, (window, dilation) in enumerate(DIL_PATTERNS):
        q_ref, kp_ref, kc_ref, kn_ref, vp_ref, vc_ref, vn_ref = in_refs[7 * p:7 * p + 7]
        half = window // (2 * dilation)
        rows_cls = block // dilation
        sub_p = min(sub, rows_cls)
        nk = sub_p + 2 * half
        rel = (lax.broadcasted_iota(jnp.int32, (sub_p, nk), 1) - half
               - lax.broadcasted_iota(jnp.int32, (sub_p, nk), 0))
        in_band = jnp.where(rel <= half, jnp.where(rel >= -half, 1, 0), 0)
        ones = jnp.ones((nk, DIL_HD), jnp.bfloat16)
        biases = []
        for st in range(rows_cls // sub_p):
            k_pos = i * rows_cls + st * sub_p - half + lax.broadcasted_iota(jnp.int32, (sub_p, nk), 1)
            valid = jnp.where(k_pos >= 0, jnp.where(k_pos < seq // dilation, in_band, 0), 0)
            biases.append(jnp.where(valid > 0, 0.0, MASK_VALUE).astype(jnp.float32))
        for cl in range(dilation):
            k = jnp.concatenate([kp_ref[0, 0, cl], kc_ref[0, 0, cl], kn_ref[0, 0, cl]], axis=0)
            v = jnp.concatenate([vp_ref[0, 0, cl], vc_ref[0, 0, cl], vn_ref[0, 0, cl]], axis=0)
            for st in range(rows_cls // sub_p):
                q = q_ref[0, 0, cl, st * sub_p:(st + 1) * sub_p, :]
                if dilation > 1:
                    tok = pl.ds(cl + dilation * st * sub_p, sub_p, stride=dilation)
                else:
                    tok = slice(st * sub_p, (st + 1) * sub_p)
                for hd in range(heads):
                    lanes = slice(hd * DIL_HD, (hd + 1) * DIL_HD)
                    kh = k[st * sub_p:st * sub_p + nk, lanes]
                    vh = v[st * sub_p:st * sub_p + nk, lanes]
                    s = lax.dot_general(q[:, lanes], kh, _NT, preferred_element_type=jnp.float32) + biases[st]
                    m = jnp.max(s, axis=-1, keepdims=True)
                    e = jnp.exp2((s - m).astype(jnp.bfloat16))
                    pv = _bf16_dot(e, jnp.concatenate([vh, ones], axis=1))
                    den = pv[:, DIL_HD:]
                    o_nat[p, hd, tok, :] = pv[:, :DIL_HD] / den
                    l_nat[p, hd, tok, :] = m + jnp.log2(den)

    for hd in range(heads):
        for c in range(block // ROW_CHUNK):
            rows = slice(c * ROW_CHUNK, (c + 1) * ROW_CHUNK)
            lses = [l_nat[p, hd, rows, :] for p in range(n)]
            mx = functools.reduce(jnp.maximum, lses)
            es = [jnp.exp2(l - mx) for l in lses]
            num = functools.reduce(jnp.add, [e * o_nat[p, hd, rows, :] for p, e in enumerate(es)])
            o_ref[0, rows, hd * DIL_HD:(hd + 1) * DIL_HD] = num / functools.reduce(jnp.add, es)


def _dil_attn(qkv_by_pattern, block=2048, sub=128, heads=2):
    _, batch, _, seq, _ = qkv_by_pattern[0].shape
    block = min(block, seq)
    width = heads * DIL_HD
    args, in_specs = [], []
    for qkv, (window, dilation) in zip(qkv_by_pattern, DIL_PATTERNS):
        half = window // (2 * dilation)
        rows_cls = block // dilation
        per_half = rows_cls // half
        last = seq // dilation // half - 1

        def spec(which, rows, row_block, dilation=dilation):
            return pl.BlockSpec((1, 1, dilation, rows, width),
                                lambda b, i, g: (which, b, 0, row_block(i), g))

        cur = lambda i: i
        prev = lambda i, per_half=per_half: jnp.maximum(i * per_half - 1, 0)
        nxt = lambda i, per_half=per_half, last=last: jnp.minimum((i + 1) * per_half, last)
        in_specs += [
            spec(0, rows_cls, cur),
            spec(1, half, prev), spec(1, rows_cls, cur), spec(1, half, nxt),
            spec(2, half, prev), spec(2, rows_cls, cur), spec(2, half, nxt),
        ]
        args += [qkv] * 7
    n = len(DIL_PATTERNS)
    return pl.pallas_call(
        functools.partial(_dil_attn_kernel, block=block, sub=sub, seq=seq),
        grid=(batch, seq // block, DIL_HEADS // heads),
        in_specs=in_specs,
        out_specs=pl.BlockSpec((1, block, width), lambda b, i, g: (b, i, g)),
        out_shape=jax.ShapeDtypeStruct((batch, seq, DIL_WIDTH), jnp.float32),
        scratch_shapes=[
            pltpu.VMEM((n, heads, block, DIL_HD), jnp.float32),
            pltpu.VMEM((n, heads, block, DIL_HD), jnp.float32),
        ],
        compiler_params=_params("parallel", "parallel", "parallel"),
        name="dil_attn",
    )(*args)


def _mix_out_kernel(x_ref, oa_ref, ob_ref, ga_ref, gb_ref, wo_ref, y_ref):
    for c in range(x_ref.shape[0] // ROW_CHUNK):
        rows = slice(c * ROW_CHUNK, (c + 1) * ROW_CHUNK)
        mix_a = _rms(oa_ref[rows, :].astype(jnp.float32), ga_ref[...]).astype(jnp.bfloat16)
        mix_b = _rms(ob_ref[rows, :], gb_ref[...]).astype(jnp.bfloat16)
        y = _bf16_dot(mix_a, wo_ref[:MLA_WIDTH, :]) + _bf16_dot(mix_b, wo_ref[MLA_WIDTH:, :])
        y_ref[rows, :] = x_ref[rows, :] + y


def _mix_out(x2d, o_a, o_b, ga, gb, wo, tm=512):
    t = x2d.shape[0]
    row = lambda i: (i, 0)
    const = lambda i: (0, 0)
    return pl.pallas_call(
        _mix_out_kernel,
        grid=(t // tm,),
        in_specs=[
            pl.BlockSpec((tm, D_MODEL), row),
            pl.BlockSpec((tm, MLA_WIDTH), row),
            pl.BlockSpec((tm, DIL_WIDTH), row),
            pl.BlockSpec((1, MLA_WIDTH), const),
            pl.BlockSpec((1, DIL_WIDTH), const),
            _resident(wo.shape),
        ],
        out_specs=pl.BlockSpec((tm, D_MODEL), row),
        out_shape=jax.ShapeDtypeStruct((t, D_MODEL), jnp.float32),
        compiler_params=_params("parallel"),
        name="mix_out",
    )(x2d, o_a, o_b, ga, gb, wo)


def _mlp_kernel(x_ref, g_ref, wu_ref, wd_ref, gf_ref, y_ref, h_scr, acc_scr):
    k = pl.program_id(1)

    @pl.when(k == 0)
    def _():
        x = x_ref[...]
        h_scr[...] = _rms(x, g_ref[...]).astype(jnp.bfloat16)
        acc_scr[...] = x

    u = _bf16_dot(h_scr[...], wu_ref[...])
    a = jnp.square(jnp.maximum(u, 0.0)).astype(jnp.bfloat16)
    acc_scr[...] += _bf16_dot(a, wd_ref[...])

    @pl.when(k == pl.num_programs(1) - 1)
    def _():
        y_ref[...] = _rms(acc_scr[...], gf_ref[...])


def _mlp(x2d, g, wu, wd, gf, tm=512, tf=1024):
    t = x2d.shape[0]
    return pl.pallas_call(
        _mlp_kernel,
        grid=(t // tm, D_FF // tf),
        in_specs=[
            pl.BlockSpec((tm, D_MODEL), lambda i, k: (i, 0)),
            pl.BlockSpec((1, D_MODEL), lambda i, k: (0, 0)),
            pl.BlockSpec((D_MODEL, tf), lambda i, k: (0, k)),
            pl.BlockSpec((tf, D_MODEL), lambda i, k: (k, 0)),
            pl.BlockSpec((1, D_MODEL), lambda i, k: (0, 0)),
        ],
        out_specs=pl.BlockSpec((tm, D_MODEL), lambda i, k: (i, 0)),
        out_shape=jax.ShapeDtypeStruct((t, D_MODEL), jnp.float32),
        scratch_shapes=[
            pltpu.VMEM((tm, D_MODEL), jnp.bfloat16),
            pltpu.VMEM((tm, D_MODEL), jnp.float32),
        ],
        compiler_params=_params("parallel", "arbitrary"),
        name="mlp",
    )(x2d, g, wu, wd, gf)


def _rope_tables(seq):
    pos = np.arange(seq, dtype=np.float64)

    def cos_sin(dim):
        inv = 1.0 / (ROPE_THETA ** (np.arange(0, dim, 2, dtype=np.float64) / dim))
        ang = pos[:, None] * inv[None, :]
        return np.cos(ang), np.sin(ang)

    c, s = cos_sin(DIL_HD)
    cos_d = np.concatenate([c, c], axis=-1)
    sin_d = np.concatenate([-s, s], axis=-1)
    c, s = cos_sin(ROPE_DIM)
    z = np.zeros_like(c)
    cos_m = np.concatenate([c, z, c, z], axis=-1)
    sin_m = np.concatenate([-s, z, s, z], axis=-1)
    return tuple(jnp.asarray(t, dtype=jnp.float32) for t in (cos_d, sin_d, cos_m, sin_m))


def _spread_rope_cols(w):
    z = jnp.zeros(w.shape[:-1] + (ROPE_HALF,), w.dtype)
    return jnp.concatenate([w[..., :ROPE_HALF], z, w[..., ROPE_HALF:], z], axis=-1)


def _prepare_weights(w_in, w_q_b, w_kv_b, w_o, w_up, w_down):
    bf = jnp.bfloat16
    n_lora = Q_LORA + KV_LORA
    wm = jnp.concatenate([w_in[:, :n_lora], _spread_rope_cols(w_in[:, n_lora:n_lora + ROPE_DIM])],
                         axis=-1).astype(bf)
    w_dil = w_in[:, n_lora + ROPE_DIM:].astype(bf)
    wq = w_q_b.reshape(Q_LORA, MLA_HEADS, NOPE_DIM + ROPE_DIM)
    wq = jnp.concatenate([wq[..., :NOPE_DIM], _spread_rope_cols(wq[..., NOPE_DIM:])], axis=-1)
    wq = wq.reshape(Q_LORA, MLA_HEADS * MLA_QK).astype(bf)
    wkv = w_kv_b.reshape(KV_LORA, MLA_HEADS, NOPE_DIM + V_DIM)
    wk = wkv[..., :NOPE_DIM].reshape(KV_LORA, MLA_HEADS * NOPE_DIM).astype(bf)
    wv = wkv[..., NOPE_DIM:].reshape(KV_LORA, MLA_WIDTH).astype(bf)
    return wm, w_dil, wq, wk, wv, w_o.astype(bf), w_up.astype(bf), w_down.astype(bf)


def _trunk(x, gains, weights):
    attn_g, qa_g, kva_g, mla_g, dil_g, mlp_g, final_g = gains
    wm, w_dil, wq, wk, wv, wo, wu, wd = weights
    batch, seq, _ = x.shape
    x2d = x.reshape(batch * seq, D_MODEL)
    cos_d, sin_d, cos_m, sin_m = _rope_tables(seq)

    q, k, v = _mla_pre(x2d, seq, attn_g, wm, qa_g, kva_g, wq, wk, wv, cos_m, sin_m)
    qkv_d = _dil_proj(x2d, batch, seq, attn_g, w_dil, cos_d, sin_d)
    o_a = _mla_attn(q.reshape(batch, seq, -1), k.reshape(batch, seq, -1), v.reshape(batch, seq, -1))
    o_b = _dil_attn(qkv_d)
    x1 = _mix_out(x2d, o_a.reshape(batch * seq, MLA_WIDTH), o_b.reshape(batch * seq, DIL_WIDTH),
                  mla_g, dil_g, wo)
    y = _mlp(x1, mlp_g, wu, wd, final_g)
    return y.reshape(batch, seq, D_MODEL)


def kernel(x_prompt, x_sample, attn_norm_g, w_in, q_a_norm_g, w_q_b, kv_a_norm_g, w_kv_b,
           mla_out_norm_g, dil_out_norm_g, w_o, mlp_norm_g, w_up, w_down, final_norm_g):
    assert w_in.shape[0] == 1, "single-layer block"
    weights = _prepare_weights(w_in[0], w_q_b[0], w_kv_b[0], w_o[0], w_up[0], w_down[0])
    gains = (attn_norm_g[0][None], q_a_norm_g[0][None], kv_a_norm_g[0][None],
             mla_out_norm_g[0][None], dil_out_norm_g[0][None], mlp_norm_g[0][None],
             final_norm_g[None])
    return (_trunk(x_prompt, gains, weights), _trunk(x_sample, gains, weights))
```

```python
import functools
import math

import numpy as np
import jax
import jax.numpy as jnp
from jax import lax
from jax.experimental import pallas as pl
from jax.experimental.pallas import tpu as pltpu

D_MODEL = 2048
EPS = 1e-6
ROPE_THETA = 10000.0
MLA_HEADS = 8
Q_LORA = 512
KV_LORA = 512
NOPE_DIM = 128
ROPE_DIM = 64
V_DIM = 128
MLA_WIDTH = MLA_HEADS * V_DIM
DIL_HEADS = 8
DIL_HD = 128
DIL_WIDTH = DIL_HEADS * DIL_HD
DIL_PATTERNS = ((128, 1), (512, 4), (2048, 16))
D_FF = 4 * D_MODEL

LANES = 128
VMEM_LIMIT_BYTES = 56 * 1024 * 1024
ROW_CHUNK = 128

MLA_QK = NOPE_DIM + LANES
ROPE_HALF = ROPE_DIM // 2
MASK_VALUE = -1e30

MLA_Q_SCALE = math.log2(math.e) / math.sqrt(NOPE_DIM + ROPE_DIM)
DIL_Q_SCALE = math.log2(math.e) / math.sqrt(DIL_HD)

_NT = (((1,), (1,)), ((), ()))


def _rms(x, g):
    return x * lax.rsqrt(jnp.mean(x * x, axis=-1, keepdims=True) + EPS) * g


def _rope128(x, cos, sin):
    return x * cos + pltpu.roll(x, LANES // 2, 1) * sin


def _bf16_dot(a, b):
    return jnp.dot(a, b, preferred_element_type=jnp.float32)


def _params(*sem):
    return pltpu.CompilerParams(dimension_semantics=sem, vmem_limit_bytes=VMEM_LIMIT_BYTES)


def _mla_pre_kernel(x_ref, g_ref, wm_ref, qg_ref, kvg_ref, wq_ref, wk_ref, wv_ref,
                    cos_ref, sin_ref, q_out, k_out, v_out):
    h = _rms(x_ref[...], g_ref[...]).astype(jnp.bfloat16)
    a = _bf16_dot(h, wm_ref[...])
    qn = _rms(a[:, :Q_LORA], qg_ref[...]).astype(jnp.bfloat16)
    kvn = _rms(a[:, Q_LORA:Q_LORA + KV_LORA], kvg_ref[...]).astype(jnp.bfloat16)
    cos, sin = cos_ref[...], sin_ref[...]
    k_rope = _rope128(a[:, Q_LORA + KV_LORA:], cos, sin).astype(jnp.bfloat16)
    q = _bf16_dot(qn, wq_ref[...]) * MLA_Q_SCALE
    k_nope = _bf16_dot(kvn, wk_ref[...])
    for hd in range(MLA_HEADS):
        lo = hd * MLA_QK
        q_out[:, lo:lo + NOPE_DIM] = q[:, lo:lo + NOPE_DIM].astype(jnp.bfloat16)
        q_out[:, lo + NOPE_DIM:lo + MLA_QK] = _rope128(
            q[:, lo + NOPE_DIM:lo + MLA_QK], cos, sin).astype(jnp.bfloat16)
        k_out[:, lo:lo + NOPE_DIM] = k_nope[:, hd * NOPE_DIM:(hd + 1) * NOPE_DIM].astype(jnp.bfloat16)
        k_out[:, lo + NOPE_DIM:lo + MLA_QK] = k_rope
    v_out[...] = _bf16_dot(kvn, wv_ref[...]).astype(jnp.bfloat16)


def _resident(shape):
    return pl.BlockSpec(shape, lambda *_: (0,) * len(shape), pipeline_mode=pl.Buffered(1))


def _mla_pre(x2d, seq, g, wm, qg, kvg, wq, wk, wv, cos_m, sin_m, tm=512):
    t = x2d.shape[0]
    pos_blocks = seq // tm
    const = lambda i: (0, 0)
    row = lambda i: (i, 0)
    pos = lambda i: (i % pos_blocks, 0)
    return pl.pallas_call(
        _mla_pre_kernel,
        grid=(t // tm,),
        in_specs=[
            pl.BlockSpec((tm, D_MODEL), row),
            pl.BlockSpec((1, D_MODEL), const),
            _resident(wm.shape),
            pl.BlockSpec((1, Q_LORA), const),
            pl.BlockSpec((1, KV_LORA), const),
            _resident(wq.shape),
            _resident(wk.shape),
            _resident(wv.shape),
            pl.BlockSpec((tm, LANES), pos),
            pl.BlockSpec((tm, LANES), pos),
        ],
        out_specs=[
            pl.BlockSpec((tm, MLA_HEADS * MLA_QK), row),
            pl.BlockSpec((tm, MLA_HEADS * MLA_QK), row),
            pl.BlockSpec((tm, MLA_WIDTH), row),
        ],
        out_shape=[
            jax.ShapeDtypeStruct((t, MLA_HEADS * MLA_QK), jnp.bfloat16),
            jax.ShapeDtypeStruct((t, MLA_HEADS * MLA_QK), jnp.bfloat16),
            jax.ShapeDtypeStruct((t, MLA_WIDTH), jnp.bfloat16),
        ],
        compiler_params=_params("parallel"),
        name="mla_pre",
    )(x2d, g, wm, qg, kvg, wq, wk, wv, cos_m, sin_m)


def _dil_proj_kernel(x_ref, g_ref, w_ref, cos_ref, sin_ref, *refs):
    n = len(DIL_PATTERNS)
    out_refs, planes = refs[:n], refs[n:]
    dils = [d for _, d in DIL_PATTERNS]
    tm = x_ref.shape[0]
    h = _rms(x_ref[...], g_ref[...]).astype(jnp.bfloat16)
    cos, sin = cos_ref[...], sin_ref[...]
    for which in range(3):
        y = _bf16_dot(h, w_ref[:, which * DIL_WIDTH:(which + 1) * DIL_WIDTH])
        for hd in range(DIL_HEADS):
            lanes = slice(hd * DIL_HD, (hd + 1) * DIL_HD)
            yh = y[:, lanes]
            if which < 2:
                yh = _rope128(yh, cos, sin)
            if which == 0:
                yh = yh * DIL_Q_SCALE
            out_refs[0][which, 0, 0, :, lanes] = yh.astype(jnp.bfloat16)
            planes[0][hd] = yh
            for lvl in range(1, n):
                d_prev, d = dils[lvl - 1], dils[lvl]
                ratio, rows_prev, rows = d // d_prev, tm // d_prev, tm // d
                for r_prev in range(d_prev):
                    for sub in range(ratio):
                        c = planes[lvl - 1][hd, pl.ds(r_prev * rows_prev + sub, rows, stride=ratio), :]
                        cls = d_prev * sub + r_prev
                        out_refs[lvl][which, 0, cls, :, lanes] = c.astype(jnp.bfloat16)
                        if lvl + 1 < n:
                            planes[lvl][hd, cls * rows:(cls + 1) * rows, :] = c


def _dil_proj(x2d, batch, seq, g, w_dil, cos_d, sin_d, tm=512):
    t = x2d.shape[0]
    tiles = seq // tm
    dils = [d for _, d in DIL_PATTERNS]
    assert dils[0] == 1 and all(b % a == 0 for a, b in zip(dils, dils[1:]))
    const = lambda i: (0, 0)
    pos = lambda i: (i % tiles, 0)
    return pl.pallas_call(
        _dil_proj_kernel,
        grid=(t // tm,),
        in_specs=[
            pl.BlockSpec((tm, D_MODEL), lambda i: (i, 0)),
            pl.BlockSpec((1, D_MODEL), const),
            _resident(w_dil.shape),
            pl.BlockSpec((tm, LANES), pos),
            pl.BlockSpec((tm, LANES), pos),
        ],
        out_specs=[
            pl.BlockSpec((3, 1, d, tm // d, DIL_WIDTH), lambda i: (0, i // tiles, 0, i % tiles, 0))
            for d in dils
        ],
        out_shape=[jax.ShapeDtypeStruct((3, batch, d, seq // d, DIL_WIDTH), jnp.bfloat16) for d in dils],
        scratch_shapes=[pltpu.VMEM((DIL_HEADS, tm, DIL_HD), jnp.float32) for _ in dils[:-1]],
        compiler_params=_params("parallel"),
        name="dil_proj",
    )(x2d, g, w_dil, cos_d, sin_d)


def _mla_attn_kernel(q_ref, k_ref, v_ref, o_ref, vext_scr, *, chunk):
    heads = vext_scr.shape[0]

    @pl.when(pl.program_id(2) == 0)
    def _():
        for hd in range(heads):
            vext_scr[hd, :, :V_DIM] = v_ref[0, :, hd * V_DIM:(hd + 1) * V_DIM]
            vext_scr[hd, :, V_DIM:] = jnp.ones((vext_scr.shape[1], V_DIM), jnp.bfloat16)

    for hd in range(heads):
        q = q_ref[0, :, hd * MLA_QK:(hd + 1) * MLA_QK]
        m = acc = None
        for c in range(k_ref.shape[1] // chunk):
            rows = slice(c * chunk, (c + 1) * chunk)
            s = lax.dot_general(q, k_ref[0, rows, hd * MLA_QK:(hd + 1) * MLA_QK], _NT,
                                preferred_element_type=jnp.float32)
            m_c = jnp.max(s, axis=-1, keepdims=True)
            m_new = m_c if m is None else jnp.maximum(m, m_c)
            p = jnp.exp2((s - m_new).astype(jnp.bfloat16))
            pv = _bf16_dot(p, vext_scr[hd, rows, :])
            acc = pv if m is None else acc * jnp.exp2(m - m_new) + pv
            m = m_new
        o = acc[:, :V_DIM] / acc[:, V_DIM:V_DIM + 1]
        o_ref[0, :, hd * V_DIM:(hd + 1) * V_DIM] = o.astype(o_ref.dtype)


def _mla_attn(q, k, v, scores_per_step=4 * 1024 * 1024, chunk=512, heads=4):
    b, s, _ = q.shape
    tq = min(s, scores_per_step // s)
    return pl.pallas_call(
        functools.partial(_mla_attn_kernel, chunk=chunk),
        grid=(b, MLA_HEADS // heads, s // tq),
        in_specs=[
            pl.BlockSpec((1, tq, heads * MLA_QK), lambda bi, h, i: (bi, i, h)),
            pl.BlockSpec((1, s, heads * MLA_QK), lambda bi, h, i: (bi, 0, h)),
            pl.BlockSpec((1, s, heads * V_DIM), lambda bi, h, i: (bi, 0, h)),
        ],
        out_specs=pl.BlockSpec((1, tq, heads * V_DIM), lambda bi, h, i: (bi, i, h)),
        out_shape=jax.ShapeDtypeStruct((b, s, MLA_WIDTH), jnp.bfloat16),
        scratch_shapes=[pltpu.VMEM((heads, s, 2 * V_DIM), jnp.bfloat16)],
        compiler_params=_params("parallel", "parallel", "arbitrary"),
        name="mla_attn",
    )(q, k, v)


def _dil_attn_kernel(*refs, block, sub, seq):
    n = len(DIL_PATTERNS)
    in_refs, o_ref, o_nat, l_nat = refs[:7 * n], refs[7 * n], refs[7 * n + 1], refs[7 * n + 2]
    i = pl.program_id(1)
    heads = o_nat.shape[1]
    for p, (window, dilation) in enumerate(DIL_PATTERNS):
        q_ref, kp_ref, kc_ref, kn_ref, vp_ref, vc_ref, vn_ref = in_refs[7 * p:7 * p + 7]
        half = window // (2 * dilation)
        rows_cls = block // dilation
        sub_p = min(sub, rows_cls)
        nk = sub_p + 2 * half
        rel = (lax.broadcasted_iota(jnp.int32, (sub_p, nk), 1) - half
               - lax.broadcasted_iota(jnp.int32, (sub_p, nk), 0))
        in_band = jnp.where(rel <= half, jnp.where(rel >= -half, 1, 0), 0)
        ones = jnp.ones((nk, DIL_HD), jnp.bfloat16)
        biases = []
        for st in range(rows_cls // sub_p):
            k_pos = i * rows_cls + st * sub_p - half + lax.broadcasted_iota(jnp.int32, (sub_p, nk), 1)
            valid = jnp.where(k_pos >= 0, jnp.where(k_pos < seq // dilation, in_band, 0), 0)
            biases.append(jnp.where(valid > 0, 0.0, MASK_VALUE).astype(jnp.float32))
        for cl in range(dilation):
            k = jnp.concatenate([kp_ref[0, 0, cl], kc_ref[0, 0, cl], kn_ref[0, 0, cl]], axis=0)
            v = jnp.concatenate([vp_ref[0, 0, cl], vc_ref[0, 0, cl], vn_ref[0, 0, cl]], axis=0)
            for st in range(rows_cls // sub_p):
                q = q_ref[0, 0, cl, st * sub_p:(st + 1) * sub_p, :]
                if dilation > 1:
                    tok = pl.ds(cl + dilation * st * sub_p, sub_p, stride=dilation)
                else:
                    tok = slice(st * sub_p, (st + 1) * sub_p)
                for hd in range(heads):
                    lanes = slice(hd * DIL_HD, (hd + 1) * DIL_HD)
                    kh = k[st * sub_p:st * sub_p + nk, lanes]
                    vh = v[st * sub_p:st * sub_p + nk, lanes]
                    s = lax.dot_general(q[:, lanes], kh, _NT, preferred_element_type=jnp.float32) + biases[st]
                    m = jnp.max(s, axis=-1, keepdims=True)
                    e = jnp.exp2((s - m).astype(jnp.bfloat16))
                    pv = _bf16_dot(e, jnp.concatenate([vh, ones], axis=1))
                    den = pv[:, DIL_HD:]
                    o_nat[p, hd, tok, :] = pv[:, :DIL_HD] / den
                    l_nat[p, hd, tok, :] = m + jnp.log2(den)

    for hd in range(heads):
        for c in range(block // ROW_CHUNK):
            rows = slice(c * ROW_CHUNK, (c + 1) * ROW_CHUNK)
            lses = [l_nat[p, hd, rows, :] for p in range(n)]
            mx = functools.reduce(jnp.maximum, lses)
            es = [jnp.exp2(l - mx) for l in lses]
            num = functools.reduce(jnp.add, [e * o_nat[p, hd, rows, :] for p, e in enumerate(es)])
            o_ref[0, rows, hd * DIL_HD:(hd + 1) * DIL_HD] = num / functools.reduce(jnp.add, es)


def _dil_attn(qkv_by_pattern, block=2048, sub=128, heads=2):
    _, batch, _, seq, _ = qkv_by_pattern[0].shape
    block = min(block, seq)
    width = heads * DIL_HD
    args, in_specs = [], []
    for qkv, (window, dilation) in zip(qkv_by_pattern, DIL_PATTERNS):
        half = window // (2 * dilation)
        rows_cls = block // dilation
        per_half = rows_cls // half
        last = seq // dilation // half - 1

        def spec(which, rows, row_block, dilation=dilation):
            return pl.BlockSpec((1, 1, dilation, rows, width),
                                lambda b, i, g: (which, b, 0, row_block(i), g))

        cur = lambda i: i
        prev = lambda i, per_half=per_half: jnp.maximum(i * per_half - 1, 0)
        nxt = lambda i, per_half=per_half, last=last: jnp.minimum((i + 1) * per_half, last)
        in_specs += [
            spec(0, rows_cls, cur),
            spec(1, half, prev), spec(1, rows_cls, cur), spec(1, half, nxt),
            spec(2, half, prev), spec(2, rows_cls, cur), spec(2, half, nxt),
        ]
        args += [qkv] * 7
    n = len(DIL_PATTERNS)
    return pl.pallas_call(
        functools.partial(_dil_attn_kernel, block=block, sub=sub, seq=seq),
        grid=(batch, seq // block, DIL_HEADS // heads),
        in_specs=in_specs,
        out_specs=pl.BlockSpec((1, block, width), lambda b, i, g: (b, i, g)),
        out_shape=jax.ShapeDtypeStruct((batch, seq, DIL_WIDTH), jnp.float32),
        scratch_shapes=[
            pltpu.VMEM((n, heads, block, DIL_HD), jnp.float32),
            pltpu.VMEM((n, heads, block, DIL_HD), jnp.float32),
        ],
        compiler_params=_params("parallel", "parallel", "parallel"),
        name="dil_attn",
    )(*args)


def _mix_out_kernel(x_ref, oa_ref, ob_ref, ga_ref, gb_ref, wo_ref, y_ref):
    for c in range(x_ref.shape[0] // ROW_CHUNK):
        rows = slice(c * ROW_CHUNK, (c + 1) * ROW_CHUNK)
        mix_a = _rms(oa_ref[rows, :].astype(jnp.float32), ga_ref[...]).astype(jnp.bfloat16)
        mix_b = _rms(ob_ref[rows, :], gb_ref[...]).astype(jnp.bfloat16)
        y = _bf16_dot(mix_a, wo_ref[:MLA_WIDTH, :]) + _bf16_dot(mix_b, wo_ref[MLA_WIDTH:, :])
        y_ref[rows, :] = x_ref[rows, :] + y


def _mix_out(x2d, o_a, o_b, ga, gb, wo, tm=512):
    t = x2d.shape[0]
    row = lambda i: (i, 0)
    const = lambda i: (0, 0)
    return pl.pallas_call(
        _mix_out_kernel,
        grid=(t // tm,),
        in_specs=[
            pl.BlockSpec((tm, D_MODEL), row),
            pl.BlockSpec((tm, MLA_WIDTH), row),
            pl.BlockSpec((tm, DIL_WIDTH), row),
            pl.BlockSpec((1, MLA_WIDTH), const),
            pl.BlockSpec((1, DIL_WIDTH), const),
            _resident(wo.shape),
        ],
        out_specs=pl.BlockSpec((tm, D_MODEL), row),
        out_shape=jax.ShapeDtypeStruct((t, D_MODEL), jnp.float32),
        compiler_params=_params("parallel"),
        name="mix_out",
    )(x2d, o_a, o_b, ga, gb, wo)


def _mlp_kernel(x_ref, g_ref, wu_ref, wd_ref, gf_ref, y_ref, h_scr, acc_scr):
    k = pl.program_id(1)

    @pl.when(k == 0)
    def _():
        x = x_ref[...]
        h_scr[...] = _rms(x, g_ref[...]).astype(jnp.bfloat16)
        acc_scr[...] = x

    u = _bf16_dot(h_scr[...], wu_ref[...])
    a = jnp.square(jnp.maximum(u, 0.0)).astype(jnp.bfloat16)
    acc_scr[...] += _bf16_dot(a, wd_ref[...])

    @pl.when(k == pl.num_programs(1) - 1)
    def _():
        y_ref[...] = _rms(acc_scr[...], gf_ref[...])


def _mlp(x2d, g, wu, wd, gf, tm=512, tf=1024):
    t = x2d.shape[0]
    return pl.pallas_call(
        _mlp_kernel,
        grid=(t // tm, D_FF // tf),
        in_specs=[
            pl.BlockSpec((tm, D_MODEL), lambda i, k: (i, 0)),
            pl.BlockSpec((1, D_MODEL), lambda i, k: (0, 0)),
            pl.BlockSpec((D_MODEL, tf), lambda i, k: (0, k)),
            pl.BlockSpec((tf, D_MODEL), lambda i, k: (k, 0)),
            pl.BlockSpec((1, D_MODEL), lambda i, k: (0, 0)),
        ],
        out_specs=pl.BlockSpec((tm, D_MODEL), lambda i, k: (i, 0)),
        out_shape=jax.ShapeDtypeStruct((t, D_MODEL), jnp.float32),
        scratch_shapes=[
            pltpu.VMEM((tm, D_MODEL), jnp.bfloat16),
            pltpu.VMEM((tm, D_MODEL), jnp.float32),
        ],
        compiler_params=_params("parallel", "arbitrary"),
        name="mlp",
    )(x2d, g, wu, wd, gf)


def _rope_tables(seq):
    pos = np.arange(seq, dtype=np.float64)

    def cos_sin(dim):
        inv = 1.0 / (ROPE_THETA ** (np.arange(0, dim, 2, dtype=np.float64) / dim))
        ang = pos[:, None] * inv[None, :]
        return np.cos(ang), np.sin(ang)

    c, s = cos_sin(DIL_HD)
    cos_d = np.concatenate([c, c], axis=-1)
    sin_d = np.concatenate([-s, s], axis=-1)
    c, s = cos_sin(ROPE_DIM)
    z = np.zeros_like(c)
    cos_m = np.concatenate([c, z, c, z], axis=-1)
    sin_m = np.concatenate([-s, z, s, z], axis=-1)
    return tuple(jnp.asarray(t, dtype=jnp.float32) for t in (cos_d, sin_d, cos_m, sin_m))


def _spread_rope_cols(w):
    z = jnp.zeros(w.shape[:-1] + (ROPE_HALF,), w.dtype)
    return jnp.concatenate([w[..., :ROPE_HALF], z, w[..., ROPE_HALF:], z], axis=-1)


def _prepare_weights(w_in, w_q_b, w_kv_b, w_o, w_up, w_down):
    bf = jnp.bfloat16
    n_lora = Q_LORA + KV_LORA
    wm = jnp.concatenate([w_in[:, :n_lora], _spread_rope_cols(w_in[:, n_lora:n_lora + ROPE_DIM])],
                         axis=-1).astype(bf)
    w_dil = w_in[:, n_lora + ROPE_DIM:].astype(bf)
    wq = w_q_b.reshape(Q_LORA, MLA_HEADS, NOPE_DIM + ROPE_DIM)
    wq = jnp.concatenate([wq[..., :NOPE_DIM], _spread_rope_cols(wq[..., NOPE_DIM:])], axis=-1)
    wq = wq.reshape(Q_LORA, MLA_HEADS * MLA_QK).astype(bf)
    wkv = w_kv_b.reshape(KV_LORA, MLA_HEADS, NOPE_DIM + V_DIM)
    wk = wkv[..., :NOPE_DIM].reshape(KV_LORA, MLA_HEADS * NOPE_DIM).astype(bf)
    wv = wkv[..., NOPE_DIM:].reshape(KV_LORA, MLA_WIDTH).astype(bf)
    return wm, w_dil, wq, wk, wv, w_o.astype(bf), w_up.astype(bf), w_down.astype(bf)


def _trunk(x, gains, weights):
    attn_g, qa_g, kva_g, mla_g, dil_g, mlp_g, final_g = gains
    wm, w_dil, wq, wk, wv, wo, wu, wd = weights
    batch, seq, _ = x.shape
    x2d = x.reshape(batch * seq, D_MODEL)
    cos_d, sin_d, cos_m, sin_m = _rope_tables(seq)

    q, k, v = _mla_pre(x2d, seq, attn_g, wm, qa_g, kva_g, wq, wk, wv, cos_m, sin_m)
    qkv_d = _dil_proj(x2d, batch, seq, attn_g, w_dil, cos_d, sin_d)
    o_a = _mla_attn(q.reshape(batch, seq, -1), k.reshape(batch, seq, -1), v.reshape(batch, seq, -1))
    o_b = _dil_attn(qkv_d)
    x1 = _mix_out(x2d, o_a.reshape(batch * seq, MLA_WIDTH), o_b.reshape(batch * seq, DIL_WIDTH),
                  mla_g, dil_g, wo)
    y = _mlp(x1, mlp_g, wu, wd, final_g)
    return y.reshape(batch, seq, D_MODEL)


def kernel(x_prompt, x_sample, attn_norm_g, w_in, q_a_norm_g, w_q_b, kv_a_norm_g, w_kv_b,
           mla_out_norm_g, dil_out_norm_g, w_o, mlp_norm_g, w_up, w_down, final_norm_g):
    assert w_in.shape[0] == 1, "single-layer block"
    weights = _prepare_weights(w_in[0], w_q_b[0], w_kv_b[0], w_o[0], w_up[0], w_down[0])
    gains = (attn_norm_g[0][None], q_a_norm_g[0][None], kv_a_norm_g[0][None],
             mla_out_norm_g[0][None], dil_out_norm_g[0][None], mlp_norm_g[0][None],
             final_norm_g[None])
    return (_trunk(x_prompt, gains, weights), _trunk(x_sample, gains, weights))
```

```python
import functools
import math

import numpy as np
import jax
import jax.numpy as jnp
from jax import lax
from jax.experimental import pallas as pl
from jax.experimental.pallas import tpu as pltpu

D_MODEL = 2048
EPS = 1e-6
ROPE_THETA = 10000.0
MLA_HEADS = 8
Q_LORA = 512
KV_LORA = 512
NOPE_DIM = 128
ROPE_DIM = 64
V_DIM = 128
MLA_WIDTH = MLA_HEADS * V_DIM
DIL_HEADS = 8
DIL_HD = 128
DIL_WIDTH = DIL_HEADS * DIL_HD
DIL_PATTERNS = ((128, 1), (512, 4), (2048, 16))
D_FF = 4 * D_MODEL

LANES = 128
VMEM_LIMIT_BYTES = 56 * 1024 * 1024
ROW_CHUNK = 128

MLA_QK = NOPE_DIM + LANES
ROPE_HALF = ROPE_DIM // 2
MASK_VALUE = -1e30

MLA_Q_SCALE = math.log2(math.e) / math.sqrt(NOPE_DIM + ROPE_DIM)
DIL_Q_SCALE = math.log2(math.e) / math.sqrt(DIL_HD)

_NT = (((1,), (1,)), ((), ()))


def _rms(x, g):
    return x * lax.rsqrt(jnp.mean(x * x, axis=-1, keepdims=True) + EPS) * g


def _rope128(x, cos, sin):
    return x * cos + pltpu.roll(x, LANES // 2, 1) * sin


def _bf16_dot(a, b):
    return jnp.dot(a, b, preferred_element_type=jnp.float32)


def _params(*sem):
    return pltpu.CompilerParams(dimension_semantics=sem, vmem_limit_bytes=VMEM_LIMIT_BYTES)


def _mla_pre_kernel(x_ref, g_ref, wm_ref, qg_ref, kvg_ref, wq_ref, wk_ref, wv_ref,
                    cos_ref, sin_ref, q_out, k_out, v_out):
    h = _rms(x_ref[...], g_ref[...]).astype(jnp.bfloat16)
    a = _bf16_dot(h, wm_ref[...])
    qn = _rms(a[:, :Q_LORA], qg_ref[...]).astype(jnp.bfloat16)
    kvn = _rms(a[:, Q_LORA:Q_LORA + KV_LORA], kvg_ref[...]).astype(jnp.bfloat16)
    cos, sin = cos_ref[...], sin_ref[...]
    k_rope = _rope128(a[:, Q_LORA + KV_LORA:], cos, sin).astype(jnp.bfloat16)
    q = _bf16_dot(qn, wq_ref[...]) * MLA_Q_SCALE
    k_nope = _bf16_dot(kvn, wk_ref[...])
    for hd in range(MLA_HEADS):
        lo = hd * MLA_QK
        q_out[:, lo:lo + NOPE_DIM] = q[:, lo:lo + NOPE_DIM].astype(jnp.bfloat16)
        q_out[:, lo + NOPE_DIM:lo + MLA_QK] = _rope128(
            q[:, lo + NOPE_DIM:lo + MLA_QK], cos, sin).astype(jnp.bfloat16)
        k_out[:, lo:lo + NOPE_DIM] = k_nope[:, hd * NOPE_DIM:(hd + 1) * NOPE_DIM].astype(jnp.bfloat16)
        k_out[:, lo + NOPE_DIM:lo + MLA_QK] = k_rope
    v_out[...] = _bf16_dot(kvn, wv_ref[...]).astype(jnp.bfloat16)


def _resident(shape):
    return pl.BlockSpec(shape, lambda *_: (0,) * len(shape), pipeline_mode=pl.Buffered(1))


def _mla_pre(x2d, seq, g, wm, qg, kvg, wq, wk, wv, cos_m, sin_m, tm=512):
    t = x2d.shape[0]
    pos_blocks = seq // tm
    const = lambda i: (0, 0)
    row = lambda i: (i, 0)
    pos = lambda i: (i % pos_blocks, 0)
    return pl.pallas_call(
        _mla_pre_kernel,
        grid=(t // tm,),
        in_specs=[
            pl.BlockSpec((tm, D_MODEL), row),
            pl.BlockSpec((1, D_MODEL), const),
            _resident(wm.shape),
            pl.BlockSpec((1, Q_LORA), const),
            pl.BlockSpec((1, KV_LORA), const),
            _resident(wq.shape),
            _resident(wk.shape),
            _resident(wv.shape),
            pl.BlockSpec((tm, LANES), pos),
            pl.BlockSpec((tm, LANES), pos),
        ],
        out_specs=[
            pl.BlockSpec((tm, MLA_HEADS * MLA_QK), row),
            pl.BlockSpec((tm, MLA_HEADS * MLA_QK), row),
            pl.BlockSpec((tm, MLA_WIDTH), row),
        ],
        out_shape=[
            jax.ShapeDtypeStruct((t, MLA_HEADS * MLA_QK), jnp.bfloat16),
            jax.ShapeDtypeStruct((t, MLA_HEADS * MLA_QK), jnp.bfloat16),
            jax.ShapeDtypeStruct((t, MLA_WIDTH), jnp.bfloat16),
        ],
        compiler_params=_params("parallel"),
        name="mla_pre",
    )(x2d, g, wm, qg, kvg, wq, wk, wv, cos_m, sin_m)


def _dil_proj_kernel(x_ref, g_ref, w_ref, cos_ref, sin_ref, *refs):
    n = len(DIL_PATTERNS)
    out_refs, planes = refs[:n], refs[n:]
    dils = [d for _, d in DIL_PATTERNS]
    tm = x_ref.shape[0]
    h = _rms(x_ref[...], g_ref[...]).astype(jnp.bfloat16)
    cos, sin = cos_ref[...], sin_ref[...]
    for which in range(3):
        y = _bf16_dot(h, w_ref[:, which * DIL_WIDTH:(which + 1) * DIL_WIDTH])
        for hd in range(DIL_HEADS):
            lanes = slice(hd * DIL_HD, (hd + 1) * DIL_HD)
            yh = y[:, lanes]
            if which < 2:
                yh = _rope128(yh, cos, sin)
            if which == 0:
                yh = yh * DIL_Q_SCALE
            out_refs[0][which, 0, 0, :, lanes] = yh.astype(jnp.bfloat16)
            planes[0][hd] = yh
            for lvl in range(1, n):
                d_prev, d = dils[lvl - 1], dils[lvl]
                ratio, rows_prev, rows = d // d_prev, tm // d_prev, tm // d
                for r_prev in range(d_prev):
                    for sub in range(ratio):
                        c = planes[lvl - 1][hd, pl.ds(r_prev * rows_prev + sub, rows, stride=ratio), :]
                        cls = d_prev * sub + r_prev
                        out_refs[lvl][which, 0, cls, :, lanes] = c.astype(jnp.bfloat16)
                        if lvl + 1 < n:
                            planes[lvl][hd, cls * rows:(cls + 1) * rows, :] = c


def _dil_proj(x2d, batch, seq, g, w_dil, cos_d, sin_d, tm=512):
    t = x2d.shape[0]
    tiles = seq // tm
    dils = [d for _, d in DIL_PATTERNS]
    assert dils[0] == 1 and all(b % a == 0 for a, b in zip(dils, dils[1:]))
    const = lambda i: (0, 0)
    pos = lambda i: (i % tiles, 0)
    return pl.pallas_call(
        _dil_proj_kernel,
        grid=(t // tm,),
        in_specs=[
            pl.BlockSpec((tm, D_MODEL), lambda i: (i, 0)),
            pl.BlockSpec((1, D_MODEL), const),
            _resident(w_dil.shape),
            pl.BlockSpec((tm, LANES), pos),
            pl.BlockSpec((tm, LANES), pos),
        ],
        out_specs=[
            pl.BlockSpec((3, 1, d, tm // d, DIL_WIDTH), lambda i: (0, i // tiles, 0, i % tiles, 0))
            for d in dils
        ],
        out_shape=[jax.ShapeDtypeStruct((3, batch, d, seq // d, DIL_WIDTH), jnp.bfloat16) for d in dils],
        scratch_shapes=[pltpu.VMEM((DIL_HEADS, tm, DIL_HD), jnp.float32) for _ in dils[:-1]],
        compiler_params=_params("parallel"),
        name="dil_proj",
    )(x2d, g, w_dil, cos_d, sin_d)


def _mla_attn_kernel(q_ref, k_ref, v_ref, o_ref, vext_scr, *, chunk):
    heads = vext_scr.shape[0]

    @pl.when(pl.program_id(2) == 0)
    def _():
        for hd in range(heads):
            vext_scr[hd, :, :V_DIM] = v_ref[0, :, hd * V_DIM:(hd + 1) * V_DIM]
            vext_scr[hd, :, V_DIM:] = jnp.ones((vext_scr.shape[1], V_DIM), jnp.bfloat16)

    for hd in range(heads):
        q = q_ref[0, :, hd * MLA_QK:(hd + 1) * MLA_QK]
        m = acc = None
        for c in range(k_ref.shape[1] // chunk):
            rows = slice(c * chunk, (c + 1) * chunk)
            s = lax.dot_general(q, k_ref[0, rows, hd * MLA_QK:(hd + 1) * MLA_QK], _NT,
                                preferred_element_type=jnp.float32)
            m_c = jnp.max(s, axis=-1, keepdims=True)
            m_new = m_c if m is None else jnp.maximum(m, m_c)
            p = jnp.exp2((s - m_new).astype(jnp.bfloat16))
            pv = _bf16_dot(p, vext_scr[hd, rows, :])
            acc = pv if m is None else acc * jnp.exp2(m - m_new) + pv
            m = m_new
        o = acc[:, :V_DIM] / acc[:, V_DIM:V_DIM + 1]
        o_ref[0, :, hd * V_DIM:(hd + 1) * V_DIM] = o.astype(o_ref.dtype)


def _mla_attn(q, k, v, scores_per_step=4 * 1024 * 1024, key_rows_per_step=8192, chunk=512):
    b, s, _ = q.shape
    tq = min(s, scores_per_step // s)
    heads = min(MLA_HEADS, key_rows_per_step // s)
    return pl.pallas_call(
        functools.partial(_mla_attn_kernel, chunk=chunk),
        grid=(b, MLA_HEADS // heads, s // tq),
        in_specs=[
            pl.BlockSpec((1, tq, heads * MLA_QK), lambda bi, h, i: (bi, i, h)),
            pl.BlockSpec((1, s, heads * MLA_QK), lambda bi, h, i: (bi, 0, h)),
            pl.BlockSpec((1, s, heads * V_DIM), lambda bi, h, i: (bi, 0, h)),
        ],
        out_specs=pl.BlockSpec((1, tq, heads * V_DIM), lambda bi, h, i: (bi, i, h)),
        out_shape=jax.ShapeDtypeStruct((b, s, MLA_WIDTH), jnp.bfloat16),
        scratch_shapes=[pltpu.VMEM((heads, s, 2 * V_DIM), jnp.bfloat16)],
        compiler_params=_params("parallel", "parallel", "arbitrary"),
        name="mla_attn",
    )(q, k, v)


def _dil_attn_kernel(*refs, block, sub, seq):
    n = len(DIL_PATTERNS)
    in_refs, o_ref, o_nat, l_nat = refs[:7 * n], refs[7 * n], refs[7 * n + 1], refs[7 * n + 2]
    i = pl.program_id(1)
    heads = o_nat.shape[1]
    for p, (window, dilation) in enumerate(DIL_PATTERNS):
        q_ref, kp_ref, kc_ref, kn_ref, vp_ref, vc_ref, vn_ref = in_refs[7 * p:7 * p + 7]
        half = window // (2 * dilation)
        rows_cls = block // dilation
        sub_p = min(sub, rows_cls)
        nk = sub_p + 2 * half
        rel = (lax.broadcasted_iota(jnp.int32, (sub_p, nk), 1) - half
               - lax.broadcasted_iota(jnp.int32, (sub_p, nk), 0))
        in_band = jnp.where(rel <= half, jnp.where(rel >= -half, 1, 0), 0)
        ones = jnp.ones((nk, DIL_HD), jnp.bfloat16)
        biases = []
        for st in range(rows_cls // sub_p):
            k_pos = i * rows_cls + st * sub_p - half + lax.broadcasted_iota(jnp.int32, (sub_p, nk), 1)
            valid = jnp.where(k_pos >= 0, jnp.where(k_pos < seq // dilation, in_band, 0), 0)
            biases.append(jnp.where(valid > 0, 0.0, MASK_VALUE).astype(jnp.float32))
        for cl in range(dilation):
            k = jnp.concatenate([kp_ref[0, 0, cl], kc_ref[0, 0, cl], kn_ref[0, 0, cl]], axis=0)
            v = jnp.concatenate([vp_ref[0, 0, cl], vc_ref[0, 0, cl], vn_ref[0, 0, cl]], axis=0)
            for st in range(rows_cls // sub_p):
                q = q_ref[0, 0, cl, st * sub_p:(st + 1) * sub_p, :]
                if dilation > 1:
                    tok = pl.ds(cl + dilation * st * sub_p, sub_p, stride=dilation)
                else:
                    tok = slice(st * sub_p, (st + 1) * sub_p)
                for hd in range(heads):
                    lanes = slice(hd * DIL_HD, (hd + 1) * DIL_HD)
                    kh = k[st * sub_p:st * sub_p + nk, lanes]
                    vh = v[st * sub_p:st * sub_p + nk, lanes]
                    s = lax.dot_general(q[:, lanes], kh, _NT, preferred_element_type=jnp.float32) + biases[st]
                    m = jnp.max(s, axis=-1, keepdims=True)
                    e = jnp.exp2((s - m).astype(jnp.bfloat16))
                    pv = _bf16_dot(e, jnp.concatenate([vh, ones], axis=1))
                    den = pv[:, DIL_HD:]
                    o_nat[p, hd, tok, :] = pv[:, :DIL_HD] / den
                    l_nat[p, hd, tok, :] = m + jnp.log2(den)

    for hd in range(heads):
        for c in range(block // ROW_CHUNK):
            rows = slice(c * ROW_CHUNK, (c + 1) * ROW_CHUNK)
            lses = [l_nat[p, hd, rows, :] for p in range(n)]
            mx = functools.reduce(jnp.maximum, lses)
            es = [jnp.exp2(l - mx) for l in lses]
            num = functools.reduce(jnp.add, [e * o_nat[p, hd, rows, :] for p, e in enumerate(es)])
            o_ref[0, rows, hd * DIL_HD:(hd + 1) * DIL_HD] = num / functools.reduce(jnp.add, es)


def _dil_attn(qkv_by_pattern, block=2048, sub=128, heads=2):
    _, batch, _, seq, _ = qkv_by_pattern[0].shape
    block = min(block, seq)
    width = heads * DIL_HD
    args, in_specs = [], []
    for qkv, (window, dilation) in zip(qkv_by_pattern, DIL_PATTERNS):
        half = window // (2 * dilation)
        rows_cls = block // dilation
        per_half = rows_cls // half
        last = seq // dilation // half - 1

        def spec(which, rows, row_block, dilation=dilation):
            return pl.BlockSpec((1, 1, dilation, rows, width),
                                lambda b, i, g: (which, b, 0, row_block(i), g))

        cur = lambda i: i
        prev = lambda i, per_half=per_half: jnp.maximum(i * per_half - 1, 0)
        nxt = lambda i, per_half=per_half, last=last: jnp.minimum((i + 1) * per_half, last)
        in_specs += [
            spec(0, rows_cls, cur),
            spec(1, half, prev), spec(1, rows_cls, cur), spec(1, half, nxt),
            spec(2, half, prev), spec(2, rows_cls, cur), spec(2, half, nxt),
        ]
        args += [qkv] * 7
    n = len(DIL_PATTERNS)
    return pl.pallas_call(
        functools.partial(_dil_attn_kernel, block=block, sub=sub, seq=seq),
        grid=(batch, seq // block, DIL_HEADS // heads),
        in_specs=in_specs,
        out_specs=pl.BlockSpec((1, block, width), lambda b, i, g: (b, i, g)),
        out_shape=jax.ShapeDtypeStruct((batch, seq, DIL_WIDTH), jnp.float32),
        scratch_shapes=[
            pltpu.VMEM((n, heads, block, DIL_HD), jnp.float32),
            pltpu.VMEM((n, heads, block, DIL_HD), jnp.float32),
        ],
        compiler_params=_params("parallel", "parallel", "parallel"),
        name="dil_attn",
    )(*args)


def _mix_out_kernel(x_ref, oa_ref, ob_ref, ga_ref, gb_ref, wo_ref, y_ref):
    for c in range(x_ref.shape[0] // ROW_CHUNK):
        rows = slice(c * ROW_CHUNK, (c + 1) * ROW_CHUNK)
        mix_a = _rms(oa_ref[rows, :].astype(jnp.float32), ga_ref[...]).astype(jnp.bfloat16)
        mix_b = _rms(ob_ref[rows, :], gb_ref[...]).astype(jnp.bfloat16)
        y = _bf16_dot(mix_a, wo_ref[:MLA_WIDTH, :]) + _bf16_dot(mix_b, wo_ref[MLA_WIDTH:, :])
        y_ref[rows, :] = x_ref[rows, :] + y


def _mix_out(x2d, o_a, o_b, ga, gb, wo, tm=512):
    t = x2d.shape[0]
    row = lambda i: (i, 0)
    const = lambda i: (0, 0)
    return pl.pallas_call(
        _mix_out_kernel,
        grid=(t // tm,),
        in_specs=[
            pl.BlockSpec((tm, D_MODEL), row),
            pl.BlockSpec((tm, MLA_WIDTH), row),
            pl.BlockSpec((tm, DIL_WIDTH), row),
            pl.BlockSpec((1, MLA_WIDTH), const),
            pl.BlockSpec((1, DIL_WIDTH), const),
            _resident(wo.shape),
        ],
        out_specs=pl.BlockSpec((tm, D_MODEL), row),
        out_shape=jax.ShapeDtypeStruct((t, D_MODEL), jnp.float32),
        compiler_params=_params("parallel"),
        name="mix_out",
    )(x2d, o_a, o_b, ga, gb, wo)


def _mlp_kernel(x_ref, g_ref, wu_ref, wd_ref, gf_ref, y_ref, h_scr, acc_scr):
    k = pl.program_id(1)

    @pl.when(k == 0)
    def _():
        x = x_ref[...]
        h_scr[...] = _rms(x, g_ref[...]).astype(jnp.bfloat16)
        acc_scr[...] = x

    u = _bf16_dot(h_scr[...], wu_ref[...])
    a = jnp.square(jnp.maximum(u, 0.0)).astype(jnp.bfloat16)
    acc_scr[...] += _bf16_dot(a, wd_ref[...])

    @pl.when(k == pl.num_programs(1) - 1)
    def _():
        y_ref[...] = _rms(acc_scr[...], gf_ref[...])


def _mlp(x2d, g, wu, wd, gf, tm=512, tf=1024):
    t = x2d.shape[0]
    return pl.pallas_call(
        _mlp_kernel,
        grid=(t // tm, D_FF // tf),
        in_specs=[
            pl.BlockSpec((tm, D_MODEL), lambda i, k: (i, 0)),
            pl.BlockSpec((1, D_MODEL), lambda i, k: (0, 0)),
            pl.BlockSpec((D_MODEL, tf), lambda i, k: (0, k)),
            pl.BlockSpec((tf, D_MODEL), lambda i, k: (k, 0)),
            pl.BlockSpec((1, D_MODEL), lambda i, k: (0, 0)),
        ],
        out_specs=pl.BlockSpec((tm, D_MODEL), lambda i, k: (i, 0)),
        out_shape=jax.ShapeDtypeStruct((t, D_MODEL), jnp.float32),
        scratch_shapes=[
            pltpu.VMEM((tm, D_MODEL), jnp.bfloat16),
            pltpu.VMEM((tm, D_MODEL), jnp.float32),
        ],
        compiler_params=_params("parallel", "arbitrary"),
        name="mlp",
    )(x2d, g, wu, wd, gf)


def _rope_tables(seq):
    pos = np.arange(seq, dtype=np.float64)

    def cos_sin(dim):
        inv = 1.0 / (ROPE_THETA ** (np.arange(0, dim, 2, dtype=np.float64) / dim))
        ang = pos[:, None] * inv[None, :]
        return np.cos(ang), np.sin(ang)

    c, s = cos_sin(DIL_HD)
    cos_d = np.concatenate([c, c], axis=-1)
    sin_d = np.concatenate([-s, s], axis=-1)
    c, s = cos_sin(ROPE_DIM)
    z = np.zeros_like(c)
    cos_m = np.concatenate([c, z, c, z], axis=-1)
    sin_m = np.concatenate([-s, z, s, z], axis=-1)
    return tuple(jnp.asarray(t, dtype=jnp.float32) for t in (cos_d, sin_d, cos_m, sin_m))


def _spread_rope_cols(w):
    z = jnp.zeros(w.shape[:-1] + (ROPE_HALF,), w.dtype)
    return jnp.concatenate([w[..., :ROPE_HALF], z, w[..., ROPE_HALF:], z], axis=-1)


def _prepare_weights(w_in, w_q_b, w_kv_b, w_o, w_up, w_down):
    bf = jnp.bfloat16
    n_lora = Q_LORA + KV_LORA
    wm = jnp.concatenate([w_in[:, :n_lora], _spread_rope_cols(w_in[:, n_lora:n_lora + ROPE_DIM])],
                         axis=-1).astype(bf)
    w_dil = w_in[:, n_lora + ROPE_DIM:].astype(bf)
    wq = w_q_b.reshape(Q_LORA, MLA_HEADS, NOPE_DIM + ROPE_DIM)
    wq = jnp.concatenate([wq[..., :NOPE_DIM], _spread_rope_cols(wq[..., NOPE_DIM:])], axis=-1)
    wq = wq.reshape(Q_LORA, MLA_HEADS * MLA_QK).astype(bf)
    wkv = w_kv_b.reshape(KV_LORA, MLA_HEADS, NOPE_DIM + V_DIM)
    wk = wkv[..., :NOPE_DIM].reshape(KV_LORA, MLA_HEADS * NOPE_DIM).astype(bf)
    wv = wkv[..., NOPE_DIM:].reshape(KV_LORA, MLA_WIDTH).astype(bf)
    return wm, w_dil, wq, wk, wv, w_o.astype(bf), w_up.astype(bf), w_down.astype(bf)


def _trunk(x, gains, weights):
    attn_g, qa_g, kva_g, mla_g, dil_g, mlp_g, final_g = gains
    wm, w_dil, wq, wk, wv, wo, wu, wd = weights
    batch, seq, _ = x.shape
    x2d = x.reshape(batch * seq, D_MODEL)
    cos_d, sin_d, cos_m, sin_m = _rope_tables(seq)

    q, k, v = _mla_pre(x2d, seq, attn_g, wm, qa_g, kva_g, wq, wk, wv, cos_m, sin_m)
    qkv_d = _dil_proj(x2d, batch, seq, attn_g, w_dil, cos_d, sin_d)
    o_a = _mla_attn(q.reshape(batch, seq, -1), k.reshape(batch, seq, -1), v.reshape(batch, seq, -1))
    o_b = _dil_attn(qkv_d)
    x1 = _mix_out(x2d, o_a.reshape(batch * seq, MLA_WIDTH), o_b.reshape(batch * seq, DIL_WIDTH),
                  mla_g, dil_g, wo)
    y = _mlp(x1, mlp_g, wu, wd, final_g)
    return y.reshape(batch, seq, D_MODEL)


def kernel(x_prompt, x_sample, attn_norm_g, w_in, q_a_norm_g, w_q_b, kv_a_norm_g, w_kv_b,
           mla_out_norm_g, dil_out_norm_g, w_o, mlp_norm_g, w_up, w_down, final_norm_g):
    assert w_in.shape[0] == 1, "single-layer block"
    weights = _prepare_weights(w_in[0], w_q_b[0], w_kv_b[0], w_o[0], w_up[0], w_down[0])
    gains = (attn_norm_g[0][None], q_a_norm_g[0][None], kv_a_norm_g[0][None],
             mla_out_norm_g[0][None], dil_out_norm_g[0][None], mlp_norm_g[0][None],
             final_norm_g[None])
    return (_trunk(x_prompt, gains, weights), _trunk(x_sample, gains, weights))
```
